```python
import math
import jax
import jax.numpy as jnp
from jax import lax
import numpy as np

D_MODEL = 2048
BATCH = 4
SEQ = 2048
DEPTH = 1
DEC_BATCH = 32
DEC_SEQ = 8
PAST_LEN = 16384
PAGE_SIZE = 128

DA_QK_DIM = 128
DA_V_DIM = 2 * DA_QK_DIM
DA_WIDTH = D_MODEL // 2
DA_HEADS = DA_WIDTH // DA_V_DIM
DA_QK_WIDTH = DA_HEADS * 2 * DA_QK_DIM
FX_DIM = 128
FX_WIDTH = D_MODEL - DA_WIDTH
FX_HEADS = FX_WIDTH // FX_DIM
MIX_WIDTH = DA_WIDTH + FX_WIDTH
IN_SPLITS = (DA_QK_WIDTH, DA_QK_WIDTH, DA_WIDTH, FX_WIDTH, FX_WIDTH, FX_WIDTH, FX_HEADS)
N_IN = sum(IN_SPLITS)
MEM_LEN = 256
MEM_HEADS = 4
MEM_DIM = D_MODEL // MEM_HEADS
PEER_HEADS = 8
PEER_TOPK = 16
N_KEYS = 128
N_EXPERTS = N_KEYS * N_KEYS
PEER_QDIM = 256
PEER_HALF = PEER_QDIM // 2
PEER_BLOCK = 128
Q_BLOCK = 128
NORM_EPS = 1e-6
SUBLN_EPS = 1e-5
FORGET_BIAS = 2.0

kernel_name = 'hybrid_diffattn_fox_peer_decoder_step'


def rms_norm(x, g, eps=NORM_EPS):
    xf = x.astype(jnp.float32)
    y = xf * lax.rsqrt(jnp.mean(xf * xf, axis=-1, keepdims=True) + eps)
    return (y * g.astype(jnp.float32)).astype(x.dtype)


def alibi_slopes(n):
    return jnp.asarray(2.0 ** (-8.0 * np.arange(1, n + 1) / n), dtype=jnp.float32)


def causal_bias(qpos, kpos, slopes):
    dist = (qpos[:, None] - kpos[None, :]).astype(jnp.float32)
    mask = dist >= 0
    bias = -slopes[:, None, None] * jnp.abs(dist)[None]
    return bias, mask


def project_heads(h, w_in, b_f):
    B, T, _ = h.shape
    z = jnp.einsum('btd,dn->btn', h, w_in)
    bounds = np.cumsum((0,) + IN_SPLITS)
    dq, dk, dv, fq, fk, fv, fl = [z[..., bounds[i]:bounds[i + 1]] for i in range(len(IN_SPLITS))]
    dq = dq.reshape(B, T, DA_HEADS, 2 * DA_QK_DIM)
    dk = dk.reshape(B, T, DA_HEADS, 2 * DA_QK_DIM)
    dv = dv.reshape(B, T, DA_HEADS, DA_V_DIM)
    fq = fq.reshape(B, T, FX_HEADS, FX_DIM)
    fk = fk.reshape(B, T, FX_HEADS, FX_DIM)
    fv = fv.reshape(B, T, FX_HEADS, FX_DIM)
    logf = jax.nn.log_sigmoid(fl.astype(jnp.float32) + b_f.astype(jnp.float32))
    return dq, dk, dv, fq, fk, fv, logf


def diff_lambda(lq1, lk1, lq2, lk2, lam_init):
    f32 = jnp.float32
    return (jnp.exp(jnp.sum(lq1.astype(f32) * lk1.astype(f32)))
            - jnp.exp(jnp.sum(lq2.astype(f32) * lk2.astype(f32))) + lam_init)


def diff_attn_core(q, k, v, bias, mask, lam, lam_init, g_sub):
    B, Q, H, _ = q.shape
    K = k.shape[1]
    q2 = q.reshape(B, Q, H, 2, DA_QK_DIM)
    k2 = k.reshape(B, K, H, 2, DA_QK_DIM)
    s = jnp.einsum('bqhmd,bkhmd->bhmqk', q2, k2).astype(jnp.float32) * DA_QK_DIM ** -0.5
    s = jnp.where(mask, s + bias[None, :, None], -jnp.inf)
    p = jax.nn.softmax(s, axis=-1)
    w = p[:, :, 0] - lam * p[:, :, 1]
    o = jnp.einsum('bhqk,bkhd->bqhd', w.astype(v.dtype), v)
    return rms_norm(o, g_sub, SUBLN_EPS) * (1.0 - lam_init)


def fox_core(q, k, v, cq, ck, mask):
    s = jnp.einsum('bqhd,bkhd->bhqk', q, k).astype(jnp.float32) * FX_DIM ** -0.5
    s = s + jnp.transpose(cq, (0, 2, 1))[..., :, None] - jnp.transpose(ck, (0, 2, 1))[..., None, :]
    s = jnp.where(mask, s, -jnp.inf)
    p = jax.nn.softmax(s, axis=-1)
    return jnp.einsum('bhqk,bkhd->bqhd', p.astype(v.dtype), v)


def prompt_mixers(dq, dk, dv, fq, fk, fv, logf, lam, lam_init, g_sub):
    B, T = dq.shape[:2]
    n_blocks = T // Q_BLOCK
    slopes = alibi_slopes(DA_HEADS)
    kpos = jnp.arange(T)
    cum = jnp.cumsum(logf, axis=1)

    def one_block(i):
        start = i * Q_BLOCK
        qpos = start + jnp.arange(Q_BLOCK)
        bias, mask = causal_bias(qpos, kpos, slopes)
        qd = lax.dynamic_slice_in_dim(dq, start, Q_BLOCK, axis=1)
        qf = lax.dynamic_slice_in_dim(fq, start, Q_BLOCK, axis=1)
        cq = lax.dynamic_slice_in_dim(cum, start, Q_BLOCK, axis=1)
        od = diff_attn_core(qd, dk, dv, bias, mask, lam, lam_init, g_sub)
        of = fox_core(qf, fk, fv, cq, cum, mask)
        return od, of

    od, of = lax.map(one_block, jnp.arange(n_blocks))
    od = jnp.moveaxis(od, 0, 1).reshape(B, T, DA_HEADS, DA_V_DIM)
    of = jnp.moveaxis(of, 0, 1).reshape(B, T, FX_HEADS, FX_DIM)
    return od, of


def sample_mixers(dq, dk, dv, fq, fk, fv, logf, pool_dk, pool_dv, pool_fk, pool_fv, pool_fl,
                  page_table, lam, lam_init, g_sub):
    S = dq.shape[1]
    past = page_table.shape[1] * PAGE_SIZE
    slopes = alibi_slopes(DA_HEADS)
    qpos = past + jnp.arange(S)
    kpos = jnp.arange(past + S)
    bias, mask = causal_bias(qpos, kpos, slopes)

    def gather(pool, pages):
        return pool[pages].reshape((past,) + pool.shape[2:])

    def one_seq(args):
        qd, kd, vd, qf, kf, vf, lf, pages = args
        kd_all = jnp.concatenate([gather(pool_dk, pages), kd], axis=0)
        vd_all = jnp.concatenate([gather(pool_dv, pages), vd], axis=0)
        kf_all = jnp.concatenate([gather(pool_fk, pages), kf], axis=0)
        vf_all = jnp.concatenate([gather(pool_fv, pages), vf], axis=0)
        lf_all = jnp.concatenate([gather(pool_fl, pages).astype(jnp.float32), lf], axis=0)
        cum = jnp.cumsum(lf_all, axis=0)
        od = diff_attn_core(qd[None], kd_all[None], vd_all[None], bias, mask, lam, lam_init, g_sub)[0]
        of = fox_core(qf[None], kf_all[None], vf_all[None], cum[None, past:], cum[None], mask)[0]
        return od, of

    return lax.map(one_seq, (dq, dk, dv, fq, fk, fv, logf, page_table))


def merge_heads(od, of, w_out):
    B, T = od.shape[:2]
    cat = jnp.concatenate([od.reshape(B, T, DA_WIDTH), of.reshape(B, T, FX_WIDTH)], axis=-1)
    return jnp.einsum('btm,md->btd', cat, w_out)


def mem_kv(mem, w_k, w_v):
    B, M, _ = mem.shape
    k = jnp.einsum('bmd,dn->bmn', mem, w_k).reshape(B, M, MEM_HEADS, MEM_DIM)
    v = jnp.einsum('bmd,dn->bmn', mem, w_v).reshape(B, M, MEM_HEADS, MEM_DIM)
    return k, v


def cross_attn(h, mem_k, mem_v, w_q, w_o):
    B, T, _ = h.shape
    q = jnp.einsum('btd,dn->btn', h, w_q).reshape(B, T, MEM_HEADS, MEM_DIM)
    s = jnp.einsum('bthd,bmhd->bhtm', q, mem_k).astype(jnp.float32) * MEM_DIM ** -0.5
    p = jax.nn.softmax(s, axis=-1).astype(mem_v.dtype)
    o = jnp.einsum('bhtm,bmhd->bthd', p, mem_v).reshape(B, T, D_MODEL)
    return jnp.einsum('btn,nd->btd', o, w_o)


def peer_ffn(h, w_q, sub_k1, sub_k2, peer_u, peer_v):
    B, T, D = h.shape
    n = B * T
    nb = -(-n // PEER_BLOCK)
    flat = jnp.pad(h.reshape(n, D), ((0, nb * PEER_BLOCK - n), (0, 0)))

    def block(xb):
        q = jnp.einsum('td,dn->tn', xb, w_q).reshape(PEER_BLOCK, PEER_HEADS, 2, PEER_HALF)
        s1 = jnp.einsum('thd,kd->thk', q[:, :, 0], sub_k1).astype(jnp.float32)
        s2 = jnp.einsum('thd,kd->thk', q[:, :, 1], sub_k2).astype(jnp.float32)
        v1, i1 = lax.top_k(s1, PEER_TOPK)
        v2, i2 = lax.top_k(s2, PEER_TOPK)
        cand = (v1[..., :, None] + v2[..., None, :]).reshape(PEER_BLOCK, PEER_HEADS, PEER_TOPK * PEER_TOPK)
        cidx = (i1[..., :, None] * N_KEYS + i2[..., None, :]).reshape(PEER_BLOCK, PEER_HEADS, PEER_TOPK * PEER_TOPK)
        top_s, pos = lax.top_k(cand, PEER_TOPK)
        eidx = jnp.take_along_axis(cidx, pos, axis=-1)
        g = jax.nn.softmax(top_s, axis=-1)
        u = peer_u[eidx]
        a = jax.nn.gelu(jnp.einsum('thkd,td->thk', u, xb).astype(jnp.float32))
        v = peer_v[eidx]
        return jnp.einsum('thk,thkd->td', (g * a).astype(v.dtype), v)

    out = lax.map(block, flat.reshape(nb, PEER_BLOCK, D))
    return out.reshape(nb * PEER_BLOCK, D)[:n].reshape(B, T, D)


def layer_tail(x, od, of, mem_k, mem_v, w_out, norm_mem, w_mem_q, w_mem_o, norm_ffn,
               w_peer_q, sub_k1, sub_k2, peer_u, peer_v):
    x = x + merge_heads(od, of, w_out)
    x = x + cross_attn(rms_norm(x, norm_mem), mem_k, mem_v, w_mem_q, w_mem_o)
    return x + peer_ffn(rms_norm(x, norm_ffn), w_peer_q, sub_k1, sub_k2, peer_u, peer_v)


def setup_inputs(seed: int = 0) -> dict:
    key = jax.random.key(seed)
    ks = iter(jax.random.split(key, 48))

    def nrm(shape, scale=1.0):
        return jax.random.normal(next(ks), shape, jnp.float32) * scale

    L = DEPTH
    n_pages = PAST_LEN // PAGE_SIZE
    n_used = DEC_BATCH * n_pages
    n_phys = n_used + max(1, n_used // 4)
    page_table = jax.random.permutation(next(ks), n_phys)[:n_used].reshape(DEC_BATCH, n_pages).astype(jnp.int32)
    return dict(
        x_prompt=nrm((BATCH, SEQ, D_MODEL)),
        x_sample=nrm((DEC_BATCH, DEC_SEQ, D_MODEL)),
        mem_prompt=nrm((BATCH, MEM_LEN, D_MODEL)),
        cache_diff_k=nrm((L, n_phys, PAGE_SIZE, DA_HEADS, 2 * DA_QK_DIM)),
        cache_diff_v=nrm((L, n_phys, PAGE_SIZE, DA_HEADS, DA_V_DIM)),
        cache_fox_k=nrm((L, n_phys, PAGE_SIZE, FX_HEADS, FX_DIM)),
        cache_fox_v=nrm((L, n_phys, PAGE_SIZE, FX_HEADS, FX_DIM)),
        cache_fox_logf=jax.nn.log_sigmoid(FORGET_BIAS + nrm((L, n_phys, PAGE_SIZE, FX_HEADS))),
        cache_mem_k=nrm((L, DEC_BATCH, MEM_LEN, MEM_HEADS, MEM_DIM)),
        cache_mem_v=nrm((L, DEC_BATCH, MEM_LEN, MEM_HEADS, MEM_DIM)),
        page_table=page_table,
        norm_attn=1.0 + nrm((L, D_MODEL), 0.02),
        w_in=nrm((L, D_MODEL, N_IN), D_MODEL ** -0.5),
        b_forget=FORGET_BIAS + nrm((L, FX_HEADS), 0.1),
        lambda_q1=nrm((L, DA_QK_DIM), 0.1),
        lambda_k1=nrm((L, DA_QK_DIM), 0.1),
        lambda_q2=nrm((L, DA_QK_DIM), 0.1),
        lambda_k2=nrm((L, DA_QK_DIM), 0.1),
        norm_sub=1.0 + nrm((L, DA_HEADS, DA_V_DIM), 0.02),
        w_out=nrm((L, MIX_WIDTH, D_MODEL), MIX_WIDTH ** -0.5),
        norm_mem=1.0 + nrm((L, D_MODEL), 0.02),
        w_mem_q=nrm((L, D_MODEL, D_MODEL), D_MODEL ** -0.5),
        w_mem_k=nrm((L, D_MODEL, D_MODEL), D_MODEL ** -0.5),
        w_mem_v=nrm((L, D_MODEL, D_MODEL), D_MODEL ** -0.5),
        w_mem_o=nrm((L, D_MODEL, D_MODEL), D_MODEL ** -0.5),
        norm_ffn=1.0 + nrm((L, D_MODEL), 0.02),
        w_peer_q=nrm((L, D_MODEL, PEER_HEADS * PEER_QDIM), D_MODEL ** -0.5),
        peer_sub_k1=nrm((L, N_KEYS, PEER_HALF), PEER_HALF ** -0.5),
        peer_sub_k2=nrm((L, N_KEYS, PEER_HALF), PEER_HALF ** -0.5),
        peer_u=nrm((L, N_EXPERTS, D_MODEL), D_MODEL ** -0.5),
        peer_v=nrm((L, N_EXPERTS, D_MODEL), PEER_HEADS ** -0.5),
        norm_final=1.0 + nrm((D_MODEL,), 0.02),
    )


def reference(x_prompt, x_sample, mem_prompt, cache_diff_k, cache_diff_v, cache_fox_k, cache_fox_v,
              cache_fox_logf, cache_mem_k, cache_mem_v, page_table, norm_attn, w_in, b_forget,
              lambda_q1, lambda_k1, lambda_q2, lambda_k2, norm_sub, w_out, norm_mem, w_mem_q,
              w_mem_k, w_mem_v, w_mem_o, norm_ffn, w_peer_q, peer_sub_k1, peer_sub_k2, peer_u,
              peer_v, norm_final):
    xp, xs = x_prompt, x_sample
    names = ('dk_p', 'dv_p', 'fk_p', 'fv_p', 'lf_p', 'mk_p', 'mv_p', 'dk_s', 'dv_s', 'fk_s', 'fv_s', 'lf_s')
    st = {n: [] for n in names}
    for l in range(DEPTH):
        lam_init = 0.8 - 0.6 * math.exp(-0.3 * l)
        lam = diff_lambda(lambda_q1[l], lambda_k1[l], lambda_q2[l], lambda_k2[l], lam_init)
        tail_w = (w_out[l], norm_mem[l], w_mem_q[l], w_mem_o[l], norm_ffn[l], w_peer_q[l],
                  peer_sub_k1[l], peer_sub_k2[l], peer_u[l], peer_v[l])

        pdq, pdk, pdv, pfq, pfk, pfv, plf = project_heads(rms_norm(xp, norm_attn[l]), w_in[l], b_forget[l])
        od, of = prompt_mixers(pdq, pdk, pdv, pfq, pfk, pfv, plf, lam, lam_init, norm_sub[l])
        pmk, pmv = mem_kv(mem_prompt, w_mem_k[l], w_mem_v[l])
        xp = layer_tail(xp, od, of, pmk, pmv, *tail_w)

        sdq, sdk, sdv, sfq, sfk, sfv, slf = project_heads(rms_norm(xs, norm_attn[l]), w_in[l], b_forget[l])
        sod, sof = sample_mixers(sdq, sdk, sdv, sfq, sfk, sfv, slf, cache_diff_k[l], cache_diff_v[l],
                                 cache_fox_k[l], cache_fox_v[l], cache_fox_logf[l], page_table,
                                 lam, lam_init, norm_sub[l])
        xs = layer_tail(xs, sod, sof, cache_mem_k[l], cache_mem_v[l], *tail_w)

        for n, a in zip(names, (pdk, pdv, pfk, pfv, plf, pmk, pmv, sdk, sdv, sfk, sfv, slf)):
            st[n].append(a)

    y_prompt = rms_norm(xp, norm_final)
    y_sample = rms_norm(xs, norm_final)
    return (y_prompt, y_sample,
            jnp.stack(st['dk_p']), jnp.stack(st['dv_p']), jnp.stack(st['fk_p']), jnp.stack(st['fv_p']),
            jnp.stack(st['lf_p']), jnp.stack(st['mk_p']), jnp.stack(st['mv_p']),
            jnp.stack(st['dk_s']), jnp.stack(st['dv_s']), jnp.stack(st['fk_s']), jnp.stack(st['fv_s']),
            jnp.stack(st['lf_s']))
```

```python
import functools
import math

import numpy as np
import jax
import jax.numpy as jnp
from jax import lax
from jax.experimental import pallas as pl
from jax.experimental.pallas import tpu as pltpu

F32 = jnp.float32
BF16 = jnp.bfloat16

D_MODEL = 2048
PAGE_SIZE = 128
DA_QK_DIM = 128
DA_V_DIM = 256
DA_HEADS = 4
DA_WIDTH = DA_HEADS * DA_V_DIM
FX_DIM = 128
FX_HEADS = 8
FX_WIDTH = FX_HEADS * FX_DIM
MEM_HEADS = 4
MEM_DIM = D_MODEL // MEM_HEADS
PEER_HEADS = 8
PEER_TOPK = 16
N_KEYS = 128
PEER_HALF = 128
NORM_EPS = 1e-6
SUBLN_EPS = 1e-5
NEG_BIG = -1e30

LANES = 128
VMEM_LIMIT = 56 * 1024 * 1024


def _cparams(sem):
    return pltpu.CompilerParams(dimension_semantics=sem, vmem_limit_bytes=VMEM_LIMIT)


def _dot_nt(a, b):
    return lax.dot_general(a, b, (((1,), (1,)), ((), ())), preferred_element_type=F32)


def _rms(x, g, eps):
    return x * lax.rsqrt(jnp.mean(x * x, axis=-1, keepdims=True) + eps) * g


def _mm_body(*refs, n_in, has_gain, has_res, logsig, eps):
    xs = refs[:n_in]
    ws = refs[n_in:2 * n_in]
    pos = 2 * n_in
    g_ref = r_ref = b_ref = None
    if has_gain:
        g_ref = refs[pos]; pos += 1
    if has_res:
        r_ref = refs[pos]; pos += 1
    if logsig:
        b_ref = refs[pos]; pos += 1
    o_ref = refs[pos]
    xb = refs[pos + 1:pos + 1 + n_in]

    @pl.when(pl.program_id(1) == 0)
    def _():
        for x_ref, s_ref in zip(xs, xb):
            x = x_ref[...]
            if has_gain:
                x = _rms(x, g_ref[...], eps)
            s_ref[...] = x.astype(BF16)

    acc = None
    for s_ref, w_ref in zip(xb, ws):
        d = jnp.dot(s_ref[...], w_ref[...], preferred_element_type=F32)
        acc = d if acc is None else acc + d
    if logsig:
        z = acc + b_ref[...]
        acc = jnp.minimum(z, 0.0) - jnp.log1p(jnp.exp(-jnp.abs(z)))
    if has_res:
        acc = acc + r_ref[...]
    o_ref[...] = acc


def _mm(xs, ws, *, gain=None, res=None, bias=None, eps=NORM_EPS, tm=1024, tn=512):
    m = xs[0].shape[0]
    n = ws[0].shape[1]
    tm = min(tm, m)
    tn = min(tn, n)
    assert m % tm == 0 and n % tn == 0
    in_specs = [pl.BlockSpec((tm, x.shape[1]), lambda i, j: (i, 0)) for x in xs]
    in_specs += [pl.BlockSpec((w.shape[0], tn), lambda i, j: (0, j)) for w in ws]
    args = list(xs) + list(ws)
    if gain is not None:
        in_specs.append(pl.BlockSpec((1, gain.shape[-1]), lambda i, j: (0, 0)))
        args.append(gain.reshape(1, -1))
    if res is not None:
        in_specs.append(pl.BlockSpec((tm, tn), lambda i, j: (i, j)))
        args.append(res)
    if bias is not None:
        in_specs.append(pl.BlockSpec((1, tn), lambda i, j: (0, j)))
        args.append(bias.reshape(1, -1))
    body = functools.partial(_mm_body, n_in=len(xs), has_gain=gain is not None,
                             has_res=res is not None, logsig=bias is not None, eps=eps)
    return pl.pallas_call(
        body,
        grid=(m // tm, n // tn),
        in_specs=in_specs,
        out_specs=pl.BlockSpec((tm, tn), lambda i, j: (i, j)),
        out_shape=jax.ShapeDtypeStruct((m, n), F32),
        scratch_shapes=[pltpu.VMEM((tm, x.shape[1]), BF16) for x in xs],
        compiler_params=_cparams(("parallel", "arbitrary")),
    )(*args)


def _lambda(lv, lam_init):
    s1 = jnp.sum(lv[0:1] * lv[1:2], axis=-1, keepdims=True)
    s2 = jnp.sum(lv[2:3] * lv[3:4], axis=-1, keepdims=True)
    return jnp.exp(s1) - jnp.exp(s2) + lam_init


def _online(s, m_old, l_old):
    m_new = jnp.maximum(m_old, jnp.max(s, axis=-1, keepdims=True))
    p = jnp.exp(s - m_new)
    alpha = jnp.exp(m_old - m_new)
    l_new = alpha * l_old + jnp.sum(p, axis=-1, keepdims=True)
    return p, alpha, m_new, l_new


def _col_to_row(col):
    n = col.shape[0]
    eye = lax.broadcasted_iota(jnp.int32, (n, n), 0) == lax.broadcasted_iota(jnp.int32, (n, n), 1)
    return jnp.sum(jnp.where(eye, col, 0.0), axis=0, keepdims=True)


def _row_to_col(row):
    n = row.shape[1]
    eye = lax.broadcasted_iota(jnp.int32, (n, n), 0) == lax.broadcasted_iota(jnp.int32, (n, n), 1)
    return jnp.sum(jnp.where(eye, row, 0.0), axis=1, keepdims=True)


def _diff_finish(a1, l1, a2, l2, lam, g, lam_init):
    o = a1 / l1 - lam * (a2 / l2)
    return _rms(o, g, SUBLN_EPS) * (1.0 - lam_init)


def _pdiff_body(slope_ref, lam_ref, q_ref, k_ref, v_ref, g_ref, o_ref, *, tq, lam_init):
    h = pl.program_id(1)
    qi = pl.program_id(2)
    slope = slope_ref[h]
    lam = _lambda(lam_ref[...], lam_init)
    scale = DA_QK_DIM ** -0.5
    q = q_ref[...]
    q1 = q[:, :DA_QK_DIM].astype(BF16)
    q2 = q[:, DA_QK_DIM:].astype(BF16)
    rc = (lax.broadcasted_iota(jnp.int32, (tq, tq), 0)
          - lax.broadcasted_iota(jnp.int32, (tq, tq), 1))

    def body(j, carry):
        m1, l1, a1, m2, l2, a2 = carry
        k = k_ref[pl.ds(pl.multiple_of(j * tq, tq), tq), :]
        v = v_ref[pl.ds(pl.multiple_of(j * tq, tq), tq), :].astype(BF16)
        dist = rc + (qi - j) * tq
        mask = dist >= 0
        bias = -slope * dist.astype(F32)
        s1 = jnp.where(mask, _dot_nt(q1, k[:, :DA_QK_DIM].astype(BF16)) * scale + bias, -jnp.inf)
        s2 = jnp.where(mask, _dot_nt(q2, k[:, DA_QK_DIM:].astype(BF16)) * scale + bias, -jnp.inf)
        p1, al1, m1, l1 = _online(s1, m1, l1)
        p2, al2, m2, l2 = _online(s2, m2, l2)
        a1 = al1 * a1 + jnp.dot(p1.astype(BF16), v, preferred_element_type=F32)
        a2 = al2 * a2 + jnp.dot(p2.astype(BF16), v, preferred_element_type=F32)
        return m1, l1, a1, m2, l2, a2

    m0 = jnp.full((tq, 1), NEG_BIG, F32)
    l0 = jnp.zeros((tq, 1), F32)
    a0 = jnp.zeros((tq, DA_V_DIM), F32)
    m1, l1, a1, m2, l2, a2 = lax.fori_loop(0, qi + 1, body, (m0, l0, a0, m0, l0, a0))
    o_ref[...] = _diff_finish(a1, l1, a2, l2, lam, g_ref[0], lam_init)


def _prompt_diff(dq, dk, dv, slopes, lam_vec, g_sub, batch, seq, lam_init, tq=256):
    nq = seq // tq
    body = functools.partial(_pdiff_body, tq=tq, lam_init=lam_init)
    return pl.pallas_call(
        body,
        grid=(batch, DA_HEADS, nq),
        in_specs=[
            pl.BlockSpec(memory_space=pltpu.SMEM),
            pl.BlockSpec((4, DA_QK_DIM), lambda b, h, i: (0, 0)),
            pl.BlockSpec((tq, DA_V_DIM), lambda b, h, i: (b * nq + i, h)),
            pl.BlockSpec((seq, DA_V_DIM), lambda b, h, i: (b, h)),
            pl.BlockSpec((seq, DA_V_DIM), lambda b, h, i: (b, h)),
            pl.BlockSpec((1, 1, DA_V_DIM), lambda b, h, i: (h, 0, 0)),
        ],
        out_specs=pl.BlockSpec((tq, DA_V_DIM), lambda b, h, i: (b * nq + i, h)),
        out_shape=jax.ShapeDtypeStruct((batch * seq, DA_WIDTH), F32),
        compiler_params=_cparams(("parallel", "parallel", "arbitrary")),
    )(slopes, lam_vec, dq, dk, dv, g_sub.reshape(DA_HEADS, 1, DA_V_DIM))


def _cumsum_body(x_ref, o_ref):
    x = x_ref[0]
    n = x.shape[1]
    lane = lax.broadcasted_iota(jnp.int32, x.shape, 1)
    s = 1
    while s < n:
        x = x + jnp.where(lane >= s, pltpu.roll(x, s, axis=1), 0.0)
        s *= 2
    o_ref[0] = x


def _cumsum_lanes(x):
    b, h, n = x.shape
    return pl.pallas_call(
        _cumsum_body,
        grid=(b,),
        in_specs=[pl.BlockSpec((1, h, n), lambda i: (i, 0, 0))],
        out_specs=pl.BlockSpec((1, h, n), lambda i: (i, 0, 0)),
        out_shape=jax.ShapeDtypeStruct((b, h, n), F32),
        compiler_params=_cparams(("parallel",)),
    )(x)


def _pfox_body(q_ref, k_ref, v_ref, c_ref, o_ref, *, tq):
    qi = pl.program_id(2)
    scale = FX_DIM ** -0.5
    q = q_ref[...].astype(BF16)
    rc = (lax.broadcasted_iota(jnp.int32, (tq, tq), 0)
          - lax.broadcasted_iota(jnp.int32, (tq, tq), 1))
    cq = _row_to_col(c_ref[0, 0, :, pl.ds(pl.multiple_of(qi * tq, tq), tq)])

    def body(j, carry):
        m, l, a = carry
        start = pl.multiple_of(j * tq, tq)
        k = k_ref[pl.ds(start, tq), :].astype(BF16)
        v = v_ref[pl.ds(start, tq), :].astype(BF16)
        ck = c_ref[0, 0, :, pl.ds(start, tq)]
        mask = rc + (qi - j) * tq >= 0
        s = jnp.where(mask, _dot_nt(q, k) * scale + cq - ck, -jnp.inf)
        p, al, m, l = _online(s, m, l)
        a = al * a + jnp.dot(p.astype(BF16), v, preferred_element_type=F32)
        return m, l, a

    m0 = jnp.full((tq, 1), NEG_BIG, F32)
    l0 = jnp.zeros((tq, 1), F32)
    a0 = jnp.zeros((tq, FX_DIM), F32)
    m, l, a = lax.fori_loop(0, qi + 1, body, (m0, l0, a0))
    o_ref[...] = a / l


def _prompt_fox(fq, fk, fv, cum, batch, seq, tq=256):
    nq = seq // tq
    return pl.pallas_call(
        functools.partial(_pfox_body, tq=tq),
        grid=(batch, FX_HEADS, nq),
        in_specs=[
            pl.BlockSpec((tq, FX_DIM), lambda b, h, i: (b * nq + i, h)),
            pl.BlockSpec((seq, FX_DIM), lambda b, h, i: (b, h)),
            pl.BlockSpec((seq, FX_DIM), lambda b, h, i: (b, h)),
            pl.BlockSpec((1, 1, 1, seq), lambda b, h, i: (b, h, 0, 0)),
        ],
        out_specs=pl.BlockSpec((tq, FX_DIM), lambda b, h, i: (b * nq + i, h)),
        out_shape=jax.ShapeDtypeStruct((batch * seq, FX_WIDTH), F32),
        compiler_params=_cparams(("parallel", "parallel", "arbitrary")),
    )(fq, fk, fv, cum.reshape(batch, FX_HEADS, 1, seq))


def _sdiff_body(pt_ref, slope_ref, lam_ref, q_ref, kn_ref, vn_ref, kc_ref, vc_ref, g_ref,
                o_ref, kpad, vpad, m_sc, l_sc, a_sc, *, dec, past, lam_init):
    p = pl.program_id(1)
    n_pages = pl.num_programs(1)
    scale = DA_QK_DIM ** -0.5
    row = lax.broadcasted_iota(jnp.int32, (dec, PAGE_SIZE), 0)
    col = lax.broadcasted_iota(jnp.int32, (dec, PAGE_SIZE), 1)

    def process(k_page, v_page, dist, mask):
        distf = dist.astype(F32)
        for h in range(DA_HEADS):
            bias = -slope_ref[h] * distf
            v = v_page[:, h * DA_V_DIM:(h + 1) * DA_V_DIM].astype(BF16)
            for mp in range(2):
                idx = 2 * h + mp
                lo = h * DA_V_DIM + mp * DA_QK_DIM
                q = q_ref[:, lo:lo + DA_QK_DIM].astype(BF16)
                k = k_page[:, lo:lo + DA_QK_DIM].astype(BF16)
                s = _dot_nt(q, k) * scale + bias
                if mask is not None:
                    s = jnp.where(mask, s, -jnp.inf)
                pr, al, m_new, l_new = _online(s, m_sc[idx], l_sc[idx])
                m_sc[idx] = m_new
                l_sc[idx] = l_new
                a_sc[idx] = al * a_sc[idx] + jnp.dot(pr.astype(BF16), v, preferred_element_type=F32)

    @pl.when(p == 0)
    def _():
        m_sc[...] = jnp.full(m_sc.shape, NEG_BIG, F32)
        l_sc[...] = jnp.zeros(l_sc.shape, F32)
        a_sc[...] = jnp.zeros(a_sc.shape, F32)
        kpad[...] = jnp.zeros(kpad.shape, F32)
        vpad[...] = jnp.zeros(vpad.shape, F32)
        kpad[0:dec, :] = kn_ref[...]
        vpad[0:dec, :] = vn_ref[...]
        process(kpad[...], vpad[...], row - col, (col <= row) & (col < dec))

    process(kc_ref[0], vc_ref[0], past + row - (p * PAGE_SIZE + col), None)

    @pl.when(p == n_pages - 1)
    def _():
        lam = _lambda(lam_ref[...], lam_init)
        for h in range(DA_HEADS):
            o = _diff_finish(a_sc[2 * h], l_sc[2 * h], a_sc[2 * h + 1], l_sc[2 * h + 1],
                             lam, g_ref[h], lam_init)
            o_ref[:, h * DA_V_DIM:(h + 1) * DA_V_DIM] = o


def _sample_diff(dq, dk_new, dv_new, pool_k, pool_v, page_table, slopes, lam_vec, g_sub,
                 lam_init, dec):
    n_seq, n_pages = page_table.shape
    n_phys = pool_k.shape[0]
    pool_k = pool_k.reshape(n_phys, PAGE_SIZE, DA_WIDTH)
    pool_v = pool_v.reshape(n_phys, PAGE_SIZE, DA_WIDTH)
    body = functools.partial(_sdiff_body, dec=dec, past=n_pages * PAGE_SIZE, lam_init=lam_init)
    tok = lambda s, p, pt: (s, 0)
    grid_spec = pltpu.PrefetchScalarGridSpec(
        num_scalar_prefetch=1,
        grid=(n_seq, n_pages),
        in_specs=[
            pl.BlockSpec(memory_space=pltpu.SMEM),
            pl.BlockSpec((4, DA_QK_DIM), lambda s, p, pt: (0, 0)),
            pl.BlockSpec((dec, DA_WIDTH), tok),
            pl.BlockSpec((dec, DA_WIDTH), tok),
            pl.BlockSpec((dec, DA_WIDTH), tok),
            pl.BlockSpec((1, PAGE_SIZE, DA_WIDTH), lambda s, p, pt: (pt[s * n_pages + p], 0, 0)),
            pl.BlockSpec((1, PAGE_SIZE, DA_WIDTH), lambda s, p, pt: (pt[s * n_pages + p], 0, 0)),
            pl.BlockSpec((DA_HEADS, 1, DA_V_DIM), lambda s, p, pt: (0, 0, 0)),
        ],
        out_specs=pl.BlockSpec((dec, DA_WIDTH), tok),
        scratch_shapes=[
            pltpu.VMEM((PAGE_SIZE, DA_WIDTH), F32),
            pltpu.VMEM((PAGE_SIZE, DA_WIDTH), F32),
            pltpu.VMEM((2 * DA_HEADS, dec, 1), F32),
            pltpu.VMEM((2 * DA_HEADS, dec, 1), F32),
            pltpu.VMEM((2 * DA_HEADS, dec, DA_V_DIM), F32),
        ],
    )
    return pl.pallas_call(
        body,
        grid_spec=grid_spec,
        out_shape=jax.ShapeDtypeStruct((n_seq * dec, DA_WIDTH), F32),
        compiler_params=_cparams(("parallel", "arbitrary")),
    )(page_table.reshape(-1), slopes, lam_vec, dq, dk_new, dv_new, pool_k, pool_v,
      g_sub.reshape(DA_HEADS, 1, DA_V_DIM))


def _split3(x):
    hi = x.astype(BF16)
    r1 = x - hi.astype(F32)
    mid = r1.astype(BF16)
    lo = (r1 - mid.astype(F32)).astype(BF16)
    return hi, mid, lo


def _sfox_body(pt_ref, q_ref, kn_ref, vn_ref, lfn_ref, kc_ref, vc_ref, lfc_ref,
               o_ref, kpad, vpad, m_sc, l_sc, a_sc, cq_sc, carry_sc, *, dec):
    p = pl.program_id(1)
    n_pages = pl.num_programs(1)
    scale = FX_DIM ** -0.5
    row = lax.broadcasted_iota(jnp.int32, (dec, PAGE_SIZE), 0)
    col = lax.broadcasted_iota(jnp.int32, (dec, PAGE_SIZE), 1)

    def process(k_page, v_page, bias_of_head, mask):
        for h in range(FX_HEADS):
            lo = h * FX_DIM
            q = q_ref[:, lo:lo + FX_DIM].astype(BF16)
            k = k_page[:, lo:lo + FX_DIM].astype(BF16)
            v = v_page[:, lo:lo + FX_DIM].astype(BF16)
            s = _dot_nt(q, k) * scale + bias_of_head(h)
            if mask is not None:
                s = jnp.where(mask, s, -jnp.inf)
            pr, al, m_new, l_new = _online(s, m_sc[h], l_sc[h])
            m_sc[h] = m_new
            l_sc[h] = l_new
            a_sc[h] = al * a_sc[h] + jnp.dot(pr.astype(BF16), v, preferred_element_type=F32)

    @pl.when(p == 0)
    def _():
        m_sc[...] = jnp.full(m_sc.shape, NEG_BIG, F32)
        l_sc[...] = jnp.zeros(l_sc.shape, F32)
        a_sc[...] = jnp.zeros(a_sc.shape, F32)
        carry_sc[...] = jnp.zeros(carry_sc.shape, F32)
        kpad[...] = jnp.zeros(kpad.shape, F32)
        vpad[...] = jnp.zeros(vpad.shape, F32)
        kpad[0:dec, :] = kn_ref[...]
        vpad[0:dec, :] = vn_ref[...]
        lfn = lfn_ref[...]
        rows = [lfn[0:1]]
        for t in range(1, dec):
            rows.append(rows[-1] + lfn[t:t + 1])
        r8 = lax.broadcasted_iota(jnp.int32, (dec, FX_HEADS), 0)
        cum = jnp.zeros((dec, FX_HEADS), F32)
        for t in range(dec):
            cum = jnp.where(r8 == t, rows[t], cum)
        cq_sc[...] = cum

        def new_bias(h):
            cqh = cq_sc[:, h:h + 1]
            ck = _col_to_row(cqh)
            ck = jnp.concatenate([ck, jnp.zeros((1, PAGE_SIZE - dec), F32)], axis=1)
            return cqh - ck

        process(kpad[...], vpad[...], new_bias, (col <= row) & (col < dec))

    lf = lfc_ref[0]
    ki = lax.broadcasted_iota(jnp.int32, (PAGE_SIZE, PAGE_SIZE), 0)
    kj = lax.broadcasted_iota(jnp.int32, (PAGE_SIZE, PAGE_SIZE), 1)
    tri = jnp.where(ki > kj, 1.0, 0.0).astype(BF16)
    after = None
    for piece in _split3(lf):
        d = jnp.dot(piece, tri, preferred_element_type=F32)
        after = d if after is None else after + d
    after = after + carry_sc[...]
    carry_sc[...] = carry_sc[...] + jnp.sum(lf, axis=1, keepdims=True)

    def page_bias(h):
        return cq_sc[:, h:h + 1] + after[h:h + 1, :]

    process(kc_ref[0], vc_ref[0], page_bias, None)

    @pl.when(p == n_pages - 1)
    def _():
        for h in range(FX_HEADS):
            o_ref[:, h * FX_DIM:(h + 1) * FX_DIM] = a_sc[h] / l_sc[h]


def _sample_fox(fq, fk_new, fv_new, lf_new, pool_k, pool_v, pool_lf_t, page_table, dec):
    n_seq, n_pages = page_table.shape
    n_phys = pool_k.shape[0]
    pool_k = pool_k.reshape(n_phys, PAGE_SIZE, FX_WIDTH)
    pool_v = pool_v.reshape(n_phys, PAGE_SIZE, FX_WIDTH)
    tok = lambda s, p, pt: (s, 0)
    page = lambda s, p, pt: (pt[s * n_pages + (n_pages - 1 - p)], 0, 0)
    grid_spec = pltpu.PrefetchScalarGridSpec(
        num_scalar_prefetch=1,
        grid=(n_seq, n_pages),
        in_specs=[
            pl.BlockSpec((dec, FX_WIDTH), tok),
            pl.BlockSpec((dec, FX_WIDTH), tok),
            pl.BlockSpec((dec, FX_WIDTH), tok),
            pl.BlockSpec((dec, FX_HEADS), tok),
            pl.BlockSpec((1, PAGE_SIZE, FX_WIDTH), page),
            pl.BlockSpec((1, PAGE_SIZE, FX_WIDTH), page),
            pl.BlockSpec((1, FX_HEADS, PAGE_SIZE), page),
        ],
        out_specs=pl.BlockSpec((dec, FX_WIDTH), tok),
        scratch_shapes=[
            pltpu.VMEM((PAGE_SIZE, FX_WIDTH), F32),
            pltpu.VMEM((PAGE_SIZE, FX_WIDTH), F32),
            pltpu.VMEM((FX_HEADS, dec, 1), F32),
            pltpu.VMEM((FX_HEADS, dec, 1), F32),
            pltpu.VMEM((FX_HEADS, dec, FX_DIM), F32),
            pltpu.VMEM((dec, FX_HEADS), F32),
            pltpu.VMEM((FX_HEADS, 1), F32),
        ],
    )
    return pl.pallas_call(
        functools.partial(_sfox_body, dec=dec),
        grid_spec=grid_spec,
        out_shape=jax.ShapeDtypeStruct((n_seq * dec, FX_WIDTH), F32),
        compiler_params=_cparams(("parallel", "arbitrary")),
    )(page_table.reshape(-1), fq, fk_new, fv_new, lf_new, pool_k, pool_v, pool_lf_t)


def _xattn_body(q_ref, k_ref, v_ref, o_ref):
    scale = MEM_DIM ** -0.5
    s = _dot_nt(q_ref[...].astype(BF16), k_ref[0].astype(BF16)) * scale
    s = s - jnp.max(s, axis=-1, keepdims=True)
    e = jnp.exp(s)
    p = e / jnp.sum(e, axis=-1, keepdims=True)
    o_ref[...] = jnp.dot(p.astype(BF16), v_ref[0].astype(BF16), preferred_element_type=F32)


def _cross_attn(q, mem_k, mem_v, tq):
    groups, mem_len, _ = mem_k.shape
    m = q.shape[0]
    nq = m // (groups * tq)
    return pl.pallas_call(
        _xattn_body,
        grid=(groups, nq, MEM_HEADS),
        in_specs=[
            pl.BlockSpec((tq, MEM_DIM), lambda g, i, h: (g * nq + i, h)),
            pl.BlockSpec((1, mem_len, MEM_DIM), lambda g, i, h: (g, 0, h)),
            pl.BlockSpec((1, mem_len, MEM_DIM), lambda g, i, h: (g, 0, h)),
        ],
        out_specs=pl.BlockSpec((tq, MEM_DIM), lambda g, i, h: (g * nq + i, h)),
        out_shape=jax.ShapeDtypeStruct((m, D_MODEL), F32),
        compiler_params=_cparams(("parallel", "parallel", "arbitrary")),
    )(q, mem_k, mem_v)


def _cand_layout():
    pos, valid = [], []
    for b in range(16):
        pos.append(b); valid.append(True)
    for a in range(1, 8):
        for b in range(8):
            pos.append(a * 16 + b); valid.append((a + 1) * (b + 1) <= PEER_TOPK)
    for a in range(8, 16):
        pos.append(a * 16); valid.append(True)
    return np.asarray(pos, np.float32)[:, None], np.asarray(valid, np.float32)[:, None]


def _top16(s, iota_f):
    rank = jnp.full(s.shape, float(PEER_TOPK), F32)
    vals = []
    work = s
    for a in range(PEER_TOPK):
        m = jnp.max(work, axis=0, keepdims=True)
        idx = jnp.min(jnp.where(work == m, iota_f, 1e9), axis=0, keepdims=True)
        hit = iota_f == idx
        rank = jnp.where(hit, float(a), rank)
        work = jnp.where(hit, -jnp.inf, work)
        vals.append(m)
    return rank, vals


def _rows_to_array(rows, n):
    tt = rows[0].shape[1]
    r = lax.broadcasted_iota(jnp.int32, (n, tt), 0)
    out = jnp.zeros((n, tt), F32)
    for a in range(n):
        out = jnp.where(r == a, rows[a], out)
    return out


def _psel_body(q_ref, k1_ref, k2_ref, pos_ref, valid_ref, w1_ref, n1_ref, w2_ref, r2_ref):
    tt = q_ref.shape[0]
    iota_f = lax.broadcasted_iota(jnp.int32, (N_KEYS, tt), 0).astype(F32)
    posb = jnp.broadcast_to(pos_ref[...], (pos_ref.shape[0], tt))
    validb = jnp.broadcast_to(valid_ref[...], posb.shape) > 0.5
    i16 = lax.broadcasted_iota(jnp.int32, (PEER_TOPK, tt), 0).astype(F32)
    for h in range(PEER_HEADS):
        lo = h * 2 * PEER_HALF
        qa = q_ref[:, lo:lo + PEER_HALF].astype(BF16)
        qb = q_ref[:, lo + PEER_HALF:lo + 2 * PEER_HALF].astype(BF16)
        s1 = _dot_nt(k1_ref[...], qa)
        s2 = _dot_nt(k2_ref[...], qb)
        rank1, v1 = _top16(s1, iota_f)
        rank2, v2 = _top16(s2, iota_f)
        v1a = _rows_to_array(v1, PEER_TOPK)
        v2a = _rows_to_array(v2, PEER_TOPK)
        blocks = [v1[0] + v2a]
        for a in range(1, 8):
            blocks.append(v1[a] + v2a[0:8])
        blocks.append(v1a[8:16] + v2[0])
        work = jnp.where(validb, jnp.concatenate(blocks, axis=0), -jnp.inf)
        top = v1[0] + v2[0]
        z = jnp.zeros((1, tt), F32)
        cnt = jnp.zeros((PEER_TOPK, tt), F32)
        for _ in range(PEER_TOPK):
            m = jnp.max(work, axis=0, keepdims=True)
            pmin = jnp.min(jnp.where(work == m, posb, 1e9), axis=0, keepdims=True)
            work = jnp.where(posb == pmin, -jnp.inf, work)
            z = z + jnp.exp(m - top)
            cnt = cnt + jnp.where(i16 == jnp.floor(pmin * (1.0 / 16.0)), 1.0, 0.0)
        n1 = jnp.zeros((N_KEYS, tt), F32)
        for a in range(PEER_TOPK):
            n1 = jnp.where(rank1 == float(a), cnt[a:a + 1], n1)
        w1_ref[h] = jnp.where(rank1 < PEER_TOPK, jnp.exp(s1 - v1[0]), 0.0)
        n1_ref[h] = n1
        w2_ref[h] = jnp.where(rank2 < PEER_TOPK, jnp.exp(s2 - v2[0]) / z, 0.0)
        r2_ref[h] = rank2


def _peer_select(q, sub_k1, sub_k2, tt=256):
    m = q.shape[0]
    tt = min(tt, m)
    pos, valid = _cand_layout()
    n_c = pos.shape[0]
    tab = jax.ShapeDtypeStruct((PEER_HEADS, N_KEYS, m), F32)
    tab_spec = pl.BlockSpec((PEER_HEADS, N_KEYS, tt), lambda i: (0, 0, i))
    return pl.pallas_call(
        _psel_body,
        grid=(m // tt,),
        in_specs=[
            pl.BlockSpec((tt, q.shape[1]), lambda i: (i, 0)),
            pl.BlockSpec((N_KEYS, PEER_HALF), lambda i: (0, 0)),
            pl.BlockSpec((N_KEYS, PEER_HALF), lambda i: (0, 0)),
            pl.BlockSpec((n_c, 1), lambda i: (0, 0)),
            pl.BlockSpec((n_c, 1), lambda i: (0, 0)),
        ],
        out_specs=[tab_spec] * 4,
        out_shape=[tab] * 4,
        compiler_params=_cparams(("parallel",)),
    )(q, sub_k1.astype(BF16), sub_k2.astype(BF16), jnp.asarray(pos), jnp.asarray(valid))


def _gelu_tanh(x):
    return 0.5 * x * (1.0 + jnp.tanh(math.sqrt(2.0 / math.pi) * (x + 0.044715 * (x * x * x))))


def _pexp_body(x_ref, gf_ref, gl_ref, u_ref, vt_ref, w1_ref, n1_ref, w2_ref, r2_ref,
               y_ref, xb_sc, wt_sc, acc_sc, *, rows_per_step, tchunk):
    e = pl.program_id(1)
    tt = x_ref.shape[0]

    @pl.when(e == 0)
    def _():
        xb_sc[...] = _rms(x_ref[...], gf_ref[...], NORM_EPS).astype(BF16)
        acc_sc[...] = jnp.zeros(acc_sc.shape, F32)

    act = _dot_nt(u_ref[...], xb_sc[...])
    for rl in range(rows_per_step):
        r = e * rows_per_step + rl
        for c in range(tt // tchunk):
            cs = slice(c * tchunk, (c + 1) * tchunk)
            gate = jnp.zeros((N_KEYS, tchunk), F32)
            for h in range(PEER_HEADS):
                w1 = w1_ref[h, pl.ds(r, 1), cs]
                n1 = n1_ref[h, pl.ds(r, 1), cs]
                gate = gate + w1 * jnp.where(r2_ref[h, :, cs] < n1, w2_ref[h, :, cs], 0.0)
            a = act[rl * N_KEYS:(rl + 1) * N_KEYS, cs]
            wt_sc[rl * N_KEYS:(rl + 1) * N_KEYS, cs] = (gate * _gelu_tanh(a)).astype(BF16)
    acc_sc[...] += jnp.dot(vt_ref[...], wt_sc[...], preferred_element_type=F32)

    @pl.when(e == pl.num_programs(1) - 1)
    def _():
        x3 = x_ref[...] + acc_sc[...].T
        y_ref[...] = _rms(x3, gl_ref[...], NORM_EPS)


def _peer_experts(x, g_ffn, g_final, u_bf, vt_bf, tabs, tt=512, rows_per_step=4):
    m, d = x.shape
    tt = min(tt, m)
    n_exp = u_bf.shape[0]
    ne = rows_per_step * N_KEYS
    tab_spec = pl.BlockSpec((PEER_HEADS, N_KEYS, tt), lambda i, e: (0, 0, i))
    body = functools.partial(_pexp_body, rows_per_step=rows_per_step, tchunk=min(256, tt))
    return pl.pallas_call(
        body,
        grid=(m // tt, n_exp // ne),
        in_specs=[
            pl.BlockSpec((tt, d), lambda i, e: (i, 0)),
            pl.BlockSpec((1, d), lambda i, e: (0, 0)),
            pl.BlockSpec((1, d), lambda i, e: (0, 0)),
            pl.BlockSpec((ne, d), lambda i, e: (e, 0)),
            pl.BlockSpec((d, ne), lambda i, e: (0, e)),
            tab_spec, tab_spec, tab_spec, tab_spec,
        ],
        out_specs=pl.BlockSpec((tt, d), lambda i, e: (i, 0)),
        out_shape=jax.ShapeDtypeStruct((m, d), F32),
        scratch_shapes=[
            pltpu.VMEM((tt, d), BF16),
            pltpu.VMEM((ne, tt), BF16),
            pltpu.VMEM((d, tt), F32),
        ],
        compiler_params=_cparams(("parallel", "arbitrary")),
    )(x, g_ffn.reshape(1, d), g_final.reshape(1, d), u_bf, vt_bf, *tabs)


def _alibi_slopes(n):
    return jnp.asarray(2.0 ** (-8.0 * np.arange(1, n + 1) / n), dtype=F32)


def _project(x, g, w_in_bf, w_fl_bf, b_forget):
    outs = [_mm([x], [w_in_bf[:, k * 1024:(k + 1) * 1024]], gain=g) for k in range(6)]
    bias = jnp.zeros((LANES,), F32).at[:FX_HEADS].set(b_forget)
    logf = _mm([x], [w_fl_bf], gain=g, bias=bias)[:, :FX_HEADS]
    return outs, logf


def _tail(x, od, of, mem_k, mem_v, xq_tile, w, u_bf, vt_bf):
    x1 = _mm([od, of], [w["out_d"], w["out_f"]], res=x)
    q = _mm([x1], [w["mem_q"]], gain=w["norm_mem"])
    o = _cross_attn(q, mem_k, mem_v, xq_tile)
    x2 = _mm([o], [w["mem_o"]], res=x1)
    pq = _mm([x2], [w["peer_q"]], gain=w["norm_ffn"])
    tabs = _peer_select(pq, w["sub_k1"], w["sub_k2"])
    return _peer_experts(x2, w["norm_ffn"], w["norm_final"], u_bf, vt_bf, tabs)


def kernel(x_prompt, x_sample, mem_prompt, cache_diff_k, cache_diff_v, cache_fox_k, cache_fox_v,
           cache_fox_logf, cache_mem_k, cache_mem_v, page_table, norm_attn, w_in, b_forget,
           lambda_q1, lambda_k1, lambda_q2, lambda_k2, norm_sub, w_out, norm_mem, w_mem_q,
           w_mem_k, w_mem_v, w_mem_o, norm_ffn, w_peer_q, peer_sub_k1, peer_sub_k2, peer_u,
           peer_v, norm_final):
    depth = w_in.shape[0]
    assert depth == 1, "the residual chain below is written for a single layer"
    l = 0
    batch, seq, d = x_prompt.shape
    n_seq, dec, _ = x_sample.shape
    mem_len = mem_prompt.shape[1]
    lam_init = 0.8 - 0.6 * math.exp(-0.3 * l)
    slopes = _alibi_slopes(DA_HEADS)
    lam_vec = jnp.stack([lambda_q1[l], lambda_k1[l], lambda_q2[l], lambda_k2[l]])

    n_main = 6 * 1024
    w_in_bf = w_in[l, :, :n_main].astype(BF16)
    w_fl_bf = jnp.pad(w_in[l, :, n_main:], ((0, 0), (0, LANES - FX_HEADS))).astype(BF16)
    w = dict(
        out_d=w_out[l, :DA_WIDTH].astype(BF16), out_f=w_out[l, DA_WIDTH:].astype(BF16),
        mem_q=w_mem_q[l].astype(BF16), mem_o=w_mem_o[l].astype(BF16),
        peer_q=w_peer_q[l].astype(BF16), norm_mem=norm_mem[l], norm_ffn=norm_ffn[l],
        norm_final=norm_final, sub_k1=peer_sub_k1[l], sub_k2=peer_sub_k2[l])
    u_bf = peer_u[l].astype(BF16)
    vt_bf = peer_v[l].T.astype(BF16)

    xp = x_prompt.reshape(batch * seq, d)
    xs = x_sample.reshape(n_seq * dec, d)

    (pdq, pdk, pdv, pfq, pfk, pfv), plf = _project(xp, norm_attn[l], w_in_bf, w_fl_bf, b_forget[l])
    cum = _cumsum_lanes(jnp.transpose(plf.reshape(batch, seq, FX_HEADS), (0, 2, 1)))
    od = _prompt_diff(pdq, pdk, pdv, slopes, lam_vec, norm_sub[l], batch, seq, lam_init)
    of = _prompt_fox(pfq, pfk, pfv, cum, batch, seq)
    mem2 = mem_prompt.reshape(batch * mem_len, d)
    pmk = _mm([mem2], [w_mem_k[l].astype(BF16)])
    pmv = _mm([mem2], [w_mem_v[l].astype(BF16)])
    y_prompt = _tail(xp, od, of, pmk.reshape(batch, mem_len, d), pmv.reshape(batch, mem_len, d),
                     512, w, u_bf, vt_bf)

    (sdq, sdk, sdv, sfq, sfk, sfv), slf = _project(xs, norm_attn[l], w_in_bf, w_fl_bf, b_forget[l])
    sod = _sample_diff(sdq, sdk, sdv, cache_diff_k[l], cache_diff_v[l], page_table, slopes,
                       lam_vec, norm_sub[l], lam_init, dec)
    pool_lf_t = jnp.transpose(cache_fox_logf[l], (0, 2, 1))
    sof = _sample_fox(sfq, sfk, sfv, slf, cache_fox_k[l], cache_fox_v[l], pool_lf_t,
                      page_table, dec)
    y_sample = _tail(xs, sod, sof, cache_mem_k[l].reshape(n_seq, mem_len, d),
                     cache_mem_v[l].reshape(n_seq, mem_len, d), dec, w, u_bf, vt_bf)

    return (y_prompt.reshape(batch, seq, d), y_sample.reshape(n_seq, dec, d),
            pdk.reshape(1, batch, seq, DA_HEADS, DA_V_DIM),
            pdv.reshape(1, batch, seq, DA_HEADS, DA_V_DIM),
            pfk.reshape(1, batch, seq, FX_HEADS, FX_DIM),
            pfv.reshape(1, batch, seq, FX_HEADS, FX_DIM),
            plf.reshape(1, batch, seq, FX_HEADS),
            pmk.reshape(1, batch, mem_len, MEM_HEADS, MEM_DIM),
            pmv.reshape(1, batch, mem_len, MEM_HEADS, MEM_DIM),
            sdk.reshape(1, n_seq, dec, DA_HEADS, DA_V_DIM),
            sdv.reshape(1, n_seq, dec, DA_HEADS, DA_V_DIM),
            sfk.reshape(1, n_seq, dec, FX_HEADS, FX_DIM),
            sfv.reshape(1, n_seq, dec, FX_HEADS, FX_DIM),
            slf.reshape(1, n_seq, dec, FX_HEADS))
```

```python
import functools
import math

import numpy as np
import jax
import jax.numpy as jnp
from jax import lax
from jax.experimental import pallas as pl
from jax.experimental.pallas import tpu as pltpu

F32 = jnp.float32
BF16 = jnp.bfloat16

D_MODEL = 2048
PAGE_SIZE = 128
DA_QK_DIM = 128
DA_V_DIM = 256
DA_HEADS = 4
DA_WIDTH = DA_HEADS * DA_V_DIM
FX_DIM = 128
FX_HEADS = 8
FX_WIDTH = FX_HEADS * FX_DIM
MEM_HEADS = 4
MEM_DIM = D_MODEL // MEM_HEADS
PEER_HEADS = 8
PEER_TOPK = 16
N_KEYS = 128
PEER_HALF = 128
NORM_EPS = 1e-6
SUBLN_EPS = 1e-5
NEG_BIG = -1e30

LANES = 128
VMEM_LIMIT = 56 * 1024 * 1024


def _cparams(sem):
    return pltpu.CompilerParams(dimension_semantics=sem, vmem_limit_bytes=VMEM_LIMIT)


def _dot_nt(a, b):
    return lax.dot_general(a, b, (((1,), (1,)), ((), ())), preferred_element_type=F32)


def _rms(x, g, eps):
    return x * lax.rsqrt(jnp.mean(x * x, axis=-1, keepdims=True) + eps) * g


def _mm_body(*refs, n_in, has_gain, has_res, logsig, eps):
    xs = refs[:n_in]
    ws = refs[n_in:2 * n_in]
    pos = 2 * n_in
    g_ref = r_ref = b_ref = None
    if has_gain:
        g_ref = refs[pos]; pos += 1
    if has_res:
        r_ref = refs[pos]; pos += 1
    if logsig:
        b_ref = refs[pos]; pos += 1
    o_ref = refs[pos]
    xb = refs[pos + 1:pos + 1 + n_in]

    @pl.when(pl.program_id(1) == 0)
    def _():
        for x_ref, s_ref in zip(xs, xb):
            x = x_ref[...]
            if has_gain:
                x = _rms(x, g_ref[...], eps)
            s_ref[...] = x.astype(BF16)

    acc = None
    for s_ref, w_ref in zip(xb, ws):
        d = jnp.dot(s_ref[...], w_ref[...], preferred_element_type=F32)
        acc = d if acc is None else acc + d
    if logsig:
        z = acc + b_ref[...]
        acc = jnp.minimum(z, 0.0) - jnp.log1p(jnp.exp(-jnp.abs(z)))
    if has_res:
        acc = acc + r_ref[...]
    o_ref[...] = acc


def _mm(xs, ws, *, gain=None, res=None, bias=None, eps=NORM_EPS, tm=1024, tn=512):
    m = xs[0].shape[0]
    n = ws[0].shape[1]
    tm = min(tm, m)
    tn = min(tn, n)
    assert m % tm == 0 and n % tn == 0
    in_specs = [pl.BlockSpec((tm, x.shape[1]), lambda i, j: (i, 0)) for x in xs]
    in_specs += [pl.BlockSpec((w.shape[0], tn), lambda i, j: (0, j)) for w in ws]
    args = list(xs) + list(ws)
    if gain is not None:
        in_specs.append(pl.BlockSpec((1, gain.shape[-1]), lambda i, j: (0, 0)))
        args.append(gain.reshape(1, -1))
    if res is not None:
        in_specs.append(pl.BlockSpec((tm, tn), lambda i, j: (i, j)))
        args.append(res)
    if bias is not None:
        in_specs.append(pl.BlockSpec((1, tn), lambda i, j: (0, j)))
        args.append(bias.reshape(1, -1))
    body = functools.partial(_mm_body, n_in=len(xs), has_gain=gain is not None,
                             has_res=res is not None, logsig=bias is not None, eps=eps)
    return pl.pallas_call(
        body,
        grid=(m // tm, n // tn),
        in_specs=in_specs,
        out_specs=pl.BlockSpec((tm, tn), lambda i, j: (i, j)),
        out_shape=jax.ShapeDtypeStruct((m, n), F32),
        scratch_shapes=[pltpu.VMEM((tm, x.shape[1]), BF16) for x in xs],
        compiler_params=_cparams(("parallel", "arbitrary")),
    )(*args)


def _lambda(lv, lam_init):
    s1 = jnp.sum(lv[0:1] * lv[1:2], axis=-1, keepdims=True)
    s2 = jnp.sum(lv[2:3] * lv[3:4], axis=-1, keepdims=True)
    return jnp.exp(s1) - jnp.exp(s2) + lam_init


def _online(s, m_old, l_old):
    m_new = jnp.maximum(m_old, jnp.max(s, axis=-1, keepdims=True))
    p = jnp.exp(s - m_new)
    alpha = jnp.exp(m_old - m_new)
    l_new = alpha * l_old + jnp.sum(p, axis=-1, keepdims=True)
    return p, alpha, m_new, l_new


def _col_to_row(col):
    n = col.shape[0]
    eye = lax.broadcasted_iota(jnp.int32, (n, n), 0) == lax.broadcasted_iota(jnp.int32, (n, n), 1)
    return jnp.sum(jnp.where(eye, col, 0.0), axis=0, keepdims=True)


def _row_to_col(row):
    n = row.shape[1]
    eye = lax.broadcasted_iota(jnp.int32, (n, n), 0) == lax.broadcasted_iota(jnp.int32, (n, n), 1)
    return jnp.sum(jnp.where(eye, row, 0.0), axis=1, keepdims=True)


def _diff_finish(a1, l1, a2, l2, lam, g, lam_init):
    o = a1 / l1 - lam * (a2 / l2)
    return _rms(o, g, SUBLN_EPS) * (1.0 - lam_init)


PDIFF_HEADS_PER_STEP = 2
PFOX_HEADS_PER_STEP = 4


def _pdiff_body(slope_ref, lam_ref, q_ref, k_ref, v_ref, g_ref, o_ref, *, tq, lam_init, hb):
    hg = pl.program_id(1)
    qi = pl.program_id(2)
    lam = _lambda(lam_ref[...], lam_init)
    scale = DA_QK_DIM ** -0.5
    nmap = 2 * hb
    qs = [(q_ref[:, i * DA_QK_DIM:(i + 1) * DA_QK_DIM] * scale).astype(BF16) for i in range(nmap)]
    slopes = [slope_ref[hg * hb + hh] for hh in range(hb)]
    rc = (lax.broadcasted_iota(jnp.int32, (tq, tq), 0)
          - lax.broadcasted_iota(jnp.int32, (tq, tq), 1))

    def body(j, carry):
        start = pl.multiple_of(j * tq, tq)
        dist = rc + (qi - j) * tq
        mask = dist >= 0
        distf = dist.astype(F32)
        out = []
        for hh in range(hb):
            bias = -slopes[hh] * distf
            v = v_ref[pl.ds(start, tq), hh * DA_V_DIM:(hh + 1) * DA_V_DIM].astype(BF16)
            for mp in range(2):
                i = 2 * hh + mp
                m, l, a = carry[i]
                k = k_ref[pl.ds(start, tq), i * DA_QK_DIM:(i + 1) * DA_QK_DIM].astype(BF16)
                s = jnp.where(mask, _dot_nt(qs[i], k) + bias, -jnp.inf)
                p, al, m, l = _online(s, m, l)
                a = al * a + jnp.dot(p.astype(BF16), v, preferred_element_type=F32)
                out.append((m, l, a))
        return tuple(out)

    init = (jnp.full((tq, 1), NEG_BIG, F32), jnp.zeros((tq, 1), F32), jnp.zeros((tq, DA_V_DIM), F32))
    res = lax.fori_loop(0, qi + 1, body, (init,) * nmap)
    for hh in range(hb):
        (_, l1, a1), (_, l2, a2) = res[2 * hh], res[2 * hh + 1]
        o_ref[:, hh * DA_V_DIM:(hh + 1) * DA_V_DIM] = _diff_finish(a1, l1, a2, l2, lam, g_ref[hh], lam_init)


def _prompt_diff(dq, dk, dv, slopes, lam_vec, g_sub, batch, seq, lam_init, tq=256,
                 hb=PDIFF_HEADS_PER_STEP):
    nq = seq // tq
    w = hb * DA_V_DIM
    body = functools.partial(_pdiff_body, tq=tq, lam_init=lam_init, hb=hb)
    return pl.pallas_call(
        body,
        grid=(batch, DA_HEADS // hb, nq),
        in_specs=[
            pl.BlockSpec(memory_space=pltpu.SMEM),
            pl.BlockSpec((4, DA_QK_DIM), lambda b, h, i: (0, 0)),
            pl.BlockSpec((tq, w), lambda b, h, i: (b * nq + i, h)),
            pl.BlockSpec((seq, w), lambda b, h, i: (b, h)),
            pl.BlockSpec((seq, w), lambda b, h, i: (b, h)),
            pl.BlockSpec((hb, 1, DA_V_DIM), lambda b, h, i: (h, 0, 0)),
        ],
        out_specs=pl.BlockSpec((tq, w), lambda b, h, i: (b * nq + i, h)),
        out_shape=jax.ShapeDtypeStruct((batch * seq, DA_WIDTH), F32),
        compiler_params=_cparams(("parallel", "parallel", "arbitrary")),
        name="prompt_diff",
    )(slopes, lam_vec, dq, dk, dv, g_sub.reshape(DA_HEADS, 1, DA_V_DIM))


def _cumsum_body(x_ref, o_ref):
    x = x_ref[0]
    n = x.shape[1]
    lane = lax.broadcasted_iota(jnp.int32, x.shape, 1)
    s = 1
    while s < n:
        x = x + jnp.where(lane >= s, pltpu.roll(x, s, axis=1), 0.0)
        s *= 2
    o_ref[0] = x


def _cumsum_lanes(x):
    b, h, n = x.shape
    return pl.pallas_call(
        _cumsum_body,
        grid=(b,),
        in_specs=[pl.BlockSpec((1, h, n), lambda i: (i, 0, 0))],
        out_specs=pl.BlockSpec((1, h, n), lambda i: (i, 0, 0)),
        out_shape=jax.ShapeDtypeStruct((b, h, n), F32),
        compiler_params=_cparams(("parallel",)),
        name="logf_cumsum",
    )(x)


def _pfox_body(q_ref, k_ref, v_ref, c_ref, o_ref, *, tq, hb):
    qi = pl.program_id(2)
    scale = FX_DIM ** -0.5
    qs = [(q_ref[:, hh * FX_DIM:(hh + 1) * FX_DIM] * scale).astype(BF16) for hh in range(hb)]
    rc = (lax.broadcasted_iota(jnp.int32, (tq, tq), 0)
          - lax.broadcasted_iota(jnp.int32, (tq, tq), 1))
    qstart = pl.multiple_of(qi * tq, tq)
    cqs = [_row_to_col(c_ref[0, hh, :, pl.ds(qstart, tq)]) for hh in range(hb)]

    def body(j, carry):
        start = pl.multiple_of(j * tq, tq)
        mask = rc + (qi - j) * tq >= 0
        out = []
        for hh in range(hb):
            m, l, a = carry[hh]
            k = k_ref[pl.ds(start, tq), hh * FX_DIM:(hh + 1) * FX_DIM].astype(BF16)
            v = v_ref[pl.ds(start, tq), hh * FX_DIM:(hh + 1) * FX_DIM].astype(BF16)
            ck = c_ref[0, hh, :, pl.ds(start, tq)]
            s = jnp.where(mask, _dot_nt(qs[hh], k) + cqs[hh] - ck, -jnp.inf)
            p, al, m, l = _online(s, m, l)
            a = al * a + jnp.dot(p.astype(BF16), v, preferred_element_type=F32)
            out.append((m, l, a))
        return tuple(out)

    init = (jnp.full((tq, 1), NEG_BIG, F32), jnp.zeros((tq, 1), F32), jnp.zeros((tq, FX_DIM), F32))
    res = lax.fori_loop(0, qi + 1, body, (init,) * hb)
    for hh in range(hb):
        _, l, a = res[hh]
        o_ref[:, hh * FX_DIM:(hh + 1) * FX_DIM] = a / l


def _prompt_fox(fq, fk, fv, cum, batch, seq, tq=256, hb=PFOX_HEADS_PER_STEP):
    nq = seq // tq
    w = hb * FX_DIM
    return pl.pallas_call(
        functools.partial(_pfox_body, tq=tq, hb=hb),
        grid=(batch, FX_HEADS // hb, nq),
        in_specs=[
            pl.BlockSpec((tq, w), lambda b, h, i: (b * nq + i, h)),
            pl.BlockSpec((seq, w), lambda b, h, i: (b, h)),
            pl.BlockSpec((seq, w), lambda b, h, i: (b, h)),
            pl.BlockSpec((1, hb, 1, seq), lambda b, h, i: (b, h, 0, 0)),
        ],
        out_specs=pl.BlockSpec((tq, w), lambda b, h, i: (b * nq + i, h)),
        out_shape=jax.ShapeDtypeStruct((batch * seq, FX_WIDTH), F32),
        compiler_params=_cparams(("parallel", "parallel", "arbitrary")),
        name="prompt_fox",
    )(fq, fk, fv, cum.reshape(batch, FX_HEADS, 1, seq))


PAGES_PER_STEP = 8
ROWS_PER_KEY = 8


def _page_rows(ref, sub):
    return ref[pl.ds(0, 1), pl.ds(sub, PAGE_SIZE, stride=ROWS_PER_KEY), :][0]


def _pad_keys(x):
    return jnp.concatenate([x, jnp.zeros((PAGE_SIZE - x.shape[0], x.shape[1]), x.dtype)], axis=0)


def _attend(q_blocks, k_blocks, v_blocks, bias, mask, m_sc, l_sc, a_sc):
    dec = q_blocks[0].shape[0]
    s = jnp.concatenate(
        [jnp.concatenate([_dot_nt(q, k) for k in ks], axis=1) for q, ks in zip(q_blocks, k_blocks)],
        axis=0) + bias
    if mask is not None:
        s = jnp.where(mask, s, -jnp.inf)
    m_old = m_sc[...]
    m_new = jnp.maximum(m_old, jnp.max(s, axis=1, keepdims=True))
    p = jnp.exp(s - m_new)
    alpha = jnp.exp(m_old - m_new)
    m_sc[...] = m_new
    l_sc[...] = alpha * l_sc[...] + jnp.sum(p, axis=1, keepdims=True)
    r = dec * (len(q_blocks) // len(v_blocks))
    outs = []
    for j, vs in enumerate(v_blocks):
        o = None
        for g, v in enumerate(vs):
            pj = p[j * r:(j + 1) * r, g * PAGE_SIZE:(g + 1) * PAGE_SIZE].astype(BF16)
            d = jnp.dot(pj, v, preferred_element_type=F32)
            o = d if o is None else o + d
        outs.append(o)
    a_sc[...] = alpha * a_sc[...] + jnp.concatenate(outs, axis=0)


def _sdiff_body(pt_ref, slope_ref, lam_ref, q_ref, kn_ref, vn_ref, g_ref, *rest,
                dec, past, lam_init, npg):
    kc, vc = rest[:npg], rest[npg:2 * npg]
    o_ref = rest[2 * npg]
    m_sc, l_sc, a_sc = rest[2 * npg + 1:]
    p = pl.program_id(1)
    nb = 2 * DA_HEADS
    scale = DA_QK_DIM ** -0.5
    qb = [(q_ref[:, i * DA_QK_DIM:(i + 1) * DA_QK_DIM] * scale).astype(BF16) for i in range(nb)]
    slope = slope_ref[...]

    def alibi(width, base):
        row = lax.broadcasted_iota(jnp.int32, (nb * dec, width), 0)
        col = lax.broadcasted_iota(jnp.int32, (nb * dec, width), 1)
        dist = base + lax.rem(row, dec) - col
        return dist, -slope * dist.astype(F32)

    @pl.when(p == 0)
    def _():
        m_sc[...] = jnp.full(m_sc.shape, NEG_BIG, F32)
        l_sc[...] = jnp.zeros(l_sc.shape, F32)
        a_sc[...] = jnp.zeros(a_sc.shape, F32)
        kb = [[_pad_keys(kn_ref[:, i * DA_QK_DIM:(i + 1) * DA_QK_DIM]).astype(BF16)] for i in range(nb)]
        vb = [[_pad_keys(vn_ref[:, h * DA_V_DIM:(h + 1) * DA_V_DIM]).astype(BF16)]
              for h in range(DA_HEADS)]
        dist, bias = alibi(PAGE_SIZE, 0)
        _attend(qb, kb, vb, bias, dist >= 0, m_sc, l_sc, a_sc)

    kb = [[_page_rows(kc[g], mp * DA_HEADS + h).astype(BF16) for g in range(npg)]
          for h in range(DA_HEADS) for mp in range(2)]
    vb = [[jnp.concatenate([_page_rows(vc[g], h), _page_rows(vc[g], DA_HEADS + h)],
                           axis=1).astype(BF16) for g in range(npg)]
          for h in range(DA_HEADS)]
    _, bias = alibi(npg * PAGE_SIZE, past - p * (npg * PAGE_SIZE))
    _attend(qb, kb, vb, bias, None, m_sc, l_sc, a_sc)

    @pl.when(p == pl.num_programs(1) - 1)
    def _():
        lam = _lambda(lam_ref[...], lam_init)
        for h in range(DA_HEADS):
            r1 = slice((2 * h) * dec, (2 * h + 1) * dec)
            r2 = slice((2 * h + 1) * dec, (2 * h + 2) * dec)
            o = _diff_finish(a_sc[r1, :], l_sc[r1, :], a_sc[r2, :], l_sc[r2, :], lam, g_ref[h], lam_init)
            o_ref[:, h * DA_V_DIM:(h + 1) * DA_V_DIM] = o


def _stored_pages_diff(pool):
    n_phys = pool.shape[0]
    x = pool.reshape(n_phys, PAGE_SIZE, DA_HEADS, 2, LANES)
    return jnp.transpose(x, (0, 1, 3, 2, 4)).reshape(n_phys, PAGE_SIZE * ROWS_PER_KEY, LANES)


def _sample_diff(dq, dk_new, dv_new, pool_k, pool_v, page_table, slopes, lam_vec, g_sub,
                 lam_init, dec, npg=PAGES_PER_STEP):
    n_seq, n_pages = page_table.shape
    npg = math.gcd(npg, n_pages)
    nb = 2 * DA_HEADS
    body = functools.partial(_sdiff_body, dec=dec, past=n_pages * PAGE_SIZE, lam_init=lam_init,
                             npg=npg)
    tok = lambda s, p, pt: (s, 0)
    const2 = lambda s, p, pt: (0, 0)

    def page(g):
        return pl.BlockSpec((1, PAGE_SIZE * ROWS_PER_KEY, LANES),
                            lambda s, p, pt: (pt[s * n_pages + p * npg + g], 0, 0))

    slope_rows = jnp.repeat(slopes, 2 * dec).reshape(nb * dec, 1)
    grid_spec = pltpu.PrefetchScalarGridSpec(
        num_scalar_prefetch=1,
        grid=(n_seq, n_pages // npg),
        in_specs=[
            pl.BlockSpec((nb * dec, 1), const2),
            pl.BlockSpec((4, DA_QK_DIM), const2),
            pl.BlockSpec((dec, DA_WIDTH), tok),
            pl.BlockSpec((dec, DA_WIDTH), tok),
            pl.BlockSpec((dec, DA_WIDTH), tok),
            pl.BlockSpec((DA_HEADS, 1, DA_V_DIM), lambda s, p, pt: (0, 0, 0)),
        ] + [page(g) for g in range(npg)] * 2,
        out_specs=pl.BlockSpec((dec, DA_WIDTH), tok),
        scratch_shapes=[
            pltpu.VMEM((nb * dec, 1), F32),
            pltpu.VMEM((nb * dec, 1), F32),
            pltpu.VMEM((nb * dec, DA_V_DIM), F32),
        ],
    )
    pages_k = _stored_pages_diff(pool_k)
    pages_v = _stored_pages_diff(pool_v)
    return pl.pallas_call(
        body,
        grid_spec=grid_spec,
        out_shape=jax.ShapeDtypeStruct((n_seq * dec, DA_WIDTH), F32),
        compiler_params=_cparams(("parallel", "arbitrary")),
        name="sample_diff",
    )(page_table.reshape(-1), slope_rows, lam_vec, dq, dk_new, dv_new,
      g_sub.reshape(DA_HEADS, 1, DA_V_DIM), *([pages_k] * npg), *([pages_v] * npg))


def _bf16_pieces(x):
    hi = x.astype(BF16).astype(F32)
    r1 = x - hi
    mid = r1.astype(BF16).astype(F32)
    lo = (r1 - mid).astype(BF16).astype(F32)
    return hi, mid, lo


def _sfox_body(pt_ref, q_ref, kn_ref, vn_ref, lfn_ref, *rest, dec, npg):
    kc, vc, lfc = rest[:npg], rest[npg:2 * npg], rest[2 * npg:3 * npg]
    o_ref = rest[3 * npg]
    m_sc, l_sc, a_sc, cq_sc, carry_sc = rest[3 * npg + 1:]
    p = pl.program_id(1)
    scale = FX_DIM ** -0.5
    qb = [(q_ref[:, h * FX_DIM:(h + 1) * FX_DIM] * scale).astype(BF16) for h in range(FX_HEADS)]

    @pl.when(p == 0)
    def _():
        m_sc[...] = jnp.full(m_sc.shape, NEG_BIG, F32)
        l_sc[...] = jnp.zeros(l_sc.shape, F32)
        a_sc[...] = jnp.zeros(a_sc.shape, F32)
        carry_sc[...] = jnp.zeros(carry_sc.shape, F32)
        lfn = lfn_ref[...]
        rows = [lfn[0:1]]
        for t in range(1, dec):
            rows.append(rows[-1] + lfn[t:t + 1])
        r8 = lax.broadcasted_iota(jnp.int32, (dec, FX_HEADS), 0)
        cum = jnp.zeros((dec, FX_HEADS), F32)
        for t in range(dec):
            cum = jnp.where(r8 == t, rows[t], cum)
        cq_sc[...] = cum
        kb = [[_pad_keys(kn_ref[:, h * FX_DIM:(h + 1) * FX_DIM]).astype(BF16)] for h in range(FX_HEADS)]
        vb = [[_pad_keys(vn_ref[:, h * FX_DIM:(h + 1) * FX_DIM]).astype(BF16)] for h in range(FX_HEADS)]
        bias = []
        for h in range(FX_HEADS):
            cqh = cum[:, h:h + 1]
            ck = jnp.concatenate([_col_to_row(cqh), jnp.zeros((1, PAGE_SIZE - dec), F32)], axis=1)
            bias.append(cqh - ck)
        row = lax.broadcasted_iota(jnp.int32, (FX_HEADS * dec, PAGE_SIZE), 0)
        col = lax.broadcasted_iota(jnp.int32, (FX_HEADS * dec, PAGE_SIZE), 1)
        _attend(qb, kb, vb, jnp.concatenate(bias, axis=0), col <= lax.rem(row, dec), m_sc, l_sc, a_sc)

    ki = lax.broadcasted_iota(jnp.int32, (PAGE_SIZE, PAGE_SIZE), 0)
    kj = lax.broadcasted_iota(jnp.int32, (PAGE_SIZE, PAGE_SIZE), 1)
    later = jnp.where(ki > kj, 1.0, 0.0).astype(BF16)
    lf = [lfc[g][0] for g in range(npg)]
    pieces = [_bf16_pieces(x) for x in lf]
    lhs = jnp.concatenate([pc[i] for i in range(3) for pc in pieces], axis=0).astype(BF16)
    prod = jnp.dot(lhs, later, preferred_element_type=F32)
    nh = FX_HEADS
    carry = carry_sc[...]
    after = []
    for g in range(npg):
        inside = (prod[g * nh:(g + 1) * nh] + prod[(npg + g) * nh:(npg + g + 1) * nh]
                  + prod[(2 * npg + g) * nh:(2 * npg + g + 1) * nh])
        after.append(inside + carry)
        carry = carry + inside[:, 0:1] + lf[g][:, 0:1]
    carry_sc[...] = carry
    cq = cq_sc[...]
    bias = jnp.concatenate(
        [cq[:, h:h + 1] + jnp.concatenate([a[h:h + 1, :] for a in after], axis=1)
         for h in range(FX_HEADS)], axis=0)
    kb = [[_page_rows(kc[g], h).astype(BF16) for g in range(npg)] for h in range(FX_HEADS)]
    vb = [[_page_rows(vc[g], h).astype(BF16) for g in range(npg)] for h in range(FX_HEADS)]
    _attend(qb, kb, vb, bias, None, m_sc, l_sc, a_sc)

    @pl.when(p == pl.num_programs(1) - 1)
    def _():
        o = a_sc[...] / l_sc[...]
        for h in range(FX_HEADS):
            o_ref[:, h * FX_DIM:(h + 1) * FX_DIM] = o[h * dec:(h + 1) * dec, :]


def _sample_fox(fq, fk_new, fv_new, lf_new, pool_k, pool_v, pool_lf_t, page_table, dec,
                npg=PAGES_PER_STEP):
    n_seq, n_pages = page_table.shape
    npg = math.gcd(npg, n_pages)
    n_phys = pool_k.shape[0]
    pages_k = pool_k.reshape(n_phys, PAGE_SIZE * ROWS_PER_KEY, LANES)
    pages_v = pool_v.reshape(n_phys, PAGE_SIZE * ROWS_PER_KEY, LANES)
    tok = lambda s, p, pt: (s, 0)

    def page(g, shape):
        return pl.BlockSpec(shape, lambda s, p, pt: (pt[s * n_pages + n_pages - 1 - (p * npg + g)], 0, 0))

    grid_spec = pltpu.PrefetchScalarGridSpec(
        num_scalar_prefetch=1,
        grid=(n_seq, n_pages // npg),
        in_specs=[
            pl.BlockSpec((dec, FX_WIDTH), tok),
            pl.BlockSpec((dec, FX_WIDTH), tok),
            pl.BlockSpec((dec, FX_WIDTH), tok),
            pl.BlockSpec((dec, FX_HEADS), tok),
        ] + [page(g, (1, PAGE_SIZE * ROWS_PER_KEY, LANES)) for g in range(npg)] * 2
          + [page(g, (1, FX_HEADS, PAGE_SIZE)) for g in range(npg)],
        out_specs=pl.BlockSpec((dec, FX_WIDTH), tok),
        scratch_shapes=[
            pltpu.VMEM((FX_HEADS * dec, 1), F32),
            pltpu.VMEM((FX_HEADS * dec, 1), F32),
            pltpu.VMEM((FX_HEADS * dec, FX_DIM), F32),
            pltpu.VMEM((dec, FX_HEADS), F32),
            pltpu.VMEM((FX_HEADS, 1), F32),
        ],
    )
    return pl.pallas_call(
        functools.partial(_sfox_body, dec=dec, npg=npg),
        grid_spec=grid_spec,
        out_shape=jax.ShapeDtypeStruct((n_seq * dec, FX_WIDTH), F32),
        compiler_params=_cparams(("parallel", "arbitrary")),
        name="sample_fox",
    )(page_table.reshape(-1), fq, fk_new, fv_new, lf_new,
      *([pages_k] * npg), *([pages_v] * npg), *([pool_lf_t] * npg))


def _xattn_body(q_ref, k_ref, v_ref, o_ref):
    scale = MEM_DIM ** -0.5
    s = _dot_nt(q_ref[...].astype(BF16), k_ref[0].astype(BF16)) * scale
    s = s - jnp.max(s, axis=-1, keepdims=True)
    e = jnp.exp(s)
    p = e / jnp.sum(e, axis=-1, keepdims=True)
    o_ref[...] = jnp.dot(p.astype(BF16), v_ref[0].astype(BF16), preferred_element_type=F32)


def _cross_attn(q, mem_k, mem_v, tq):
    groups, mem_len, _ = mem_k.shape
    m = q.shape[0]
    nq = m // (groups * tq)
    return pl.pallas_call(
        _xattn_body,
        grid=(groups, nq, MEM_HEADS),
        in_specs=[
            pl.BlockSpec((tq, MEM_DIM), lambda g, i, h: (g * nq + i, h)),
            pl.BlockSpec((1, mem_len, MEM_DIM), lambda g, i, h: (g, 0, h)),
            pl.BlockSpec((1, mem_len, MEM_DIM), lambda g, i, h: (g, 0, h)),
        ],
        out_specs=pl.BlockSpec((tq, MEM_DIM), lambda g, i, h: (g * nq + i, h)),
        out_shape=jax.ShapeDtypeStruct((m, D_MODEL), F32),
        compiler_params=_cparams(("parallel", "parallel", "arbitrary")),
    )(q, mem_k, mem_v)


def _cand_layout():
    pos, valid = [], []
    for b in range(16):
        pos.append(b); valid.append(True)
    for a in range(1, 8):
        for b in range(8):
            pos.append(a * 16 + b); valid.append((a + 1) * (b + 1) <= PEER_TOPK)
    for a in range(8, 16):
        pos.append(a * 16); valid.append(True)
    return np.asarray(pos, np.float32)[:, None], np.asarray(valid, np.float32)[:, None]


def _top16(s, iota_f):
    rank = jnp.full(s.shape, float(PEER_TOPK), F32)
    vals = []
    work = s
    for a in range(PEER_TOPK):
        m = jnp.max(work, axis=0, keepdims=True)
        idx = jnp.min(jnp.where(work == m, iota_f, 1e9), axis=0, keepdims=True)
        hit = iota_f == idx
        rank = jnp.where(hit, float(a), rank)
        work = jnp.where(hit, -jnp.inf, work)
        vals.append(m)
    return rank, vals


def _rows_to_array(rows, n):
    tt = rows[0].shape[1]
    r = lax.broadcasted_iota(jnp.int32, (n, tt), 0)
    out = jnp.zeros((n, tt), F32)
    for a in range(n):
        out = jnp.where(r == a, rows[a], out)
    return out


def _psel_body(q_ref, k1_ref, k2_ref, pos_ref, valid_ref, w1_ref, n1_ref, w2_ref, r2_ref):
    tt = q_ref.shape[0]
    iota_f = lax.broadcasted_iota(jnp.int32, (N_KEYS, tt), 0).astype(F32)
    posb = jnp.broadcast_to(pos_ref[...], (pos_ref.shape[0], tt))
    validb = jnp.broadcast_to(valid_ref[...], posb.shape) > 0.5
    i16 = lax.broadcasted_iota(jnp.int32, (PEER_TOPK, tt), 0).astype(F32)
    for h in range(PEER_HEADS):
        lo = h * 2 * PEER_HALF
        qa = q_ref[:, lo:lo + PEER_HALF].astype(BF16)
        qb = q_ref[:, lo + PEER_HALF:lo + 2 * PEER_HALF].astype(BF16)
        s1 = _dot_nt(k1_ref[...], qa)
        s2 = _dot_nt(k2_ref[...], qb)
        rank1, v1 = _top16(s1, iota_f)
        rank2, v2 = _top16(s2, iota_f)
        v1a = _rows_to_array(v1, PEER_TOPK)
        v2a = _rows_to_array(v2, PEER_TOPK)
        blocks = [v1[0] + v2a]
        for a in range(1, 8):
            blocks.append(v1[a] + v2a[0:8])
        blocks.append(v1a[8:16] + v2[0])
        work = jnp.where(validb, jnp.concatenate(blocks, axis=0), -jnp.inf)
        top = v1[0] + v2[0]
        z = jnp.zeros((1, tt), F32)
        cnt = jnp.zeros((PEER_TOPK, tt), F32)
        for _ in range(PEER_TOPK):
            m = jnp.max(work, axis=0, keepdims=True)
            pmin = jnp.min(jnp.where(work == m, posb, 1e9), axis=0, keepdims=True)
            work = jnp.where(posb == pmin, -jnp.inf, work)
            z = z + jnp.exp(m - top)
            cnt = cnt + jnp.where(i16 == jnp.floor(pmin * (1.0 / 16.0)), 1.0, 0.0)
        n1 = jnp.zeros((N_KEYS, tt), F32)
        for a in range(PEER_TOPK):
            n1 = jnp.where(rank1 == float(a), cnt[a:a + 1], n1)
        w1_ref[h] = jnp.where(rank1 < PEER_TOPK, jnp.exp(s1 - v1[0]), 0.0)
        n1_ref[h] = n1
        w2_ref[h] = jnp.where(rank2 < PEER_TOPK, jnp.exp(s2 - v2[0]) / z, 0.0)
        r2_ref[h] = rank2


def _peer_select(q, sub_k1, sub_k2, tt=256):
    m = q.shape[0]
    tt = min(tt, m)
    pos, valid = _cand_layout()
    n_c = pos.shape[0]
    tab = jax.ShapeDtypeStruct((PEER_HEADS, N_KEYS, m), F32)
    tab_spec = pl.BlockSpec((PEER_HEADS, N_KEYS, tt), lambda i: (0, 0, i))
    return pl.pallas_call(
        _psel_body,
        grid=(m // tt,),
        in_specs=[
            pl.BlockSpec((tt, q.shape[1]), lambda i: (i, 0)),
            pl.BlockSpec((N_KEYS, PEER_HALF), lambda i: (0, 0)),
            pl.BlockSpec((N_KEYS, PEER_HALF), lambda i: (0, 0)),
            pl.BlockSpec((n_c, 1), lambda i: (0, 0)),
            pl.BlockSpec((n_c, 1), lambda i: (0, 0)),
        ],
        out_specs=[tab_spec] * 4,
        out_shape=[tab] * 4,
        compiler_params=_cparams(("parallel",)),
        name="peer_select",
    )(q, sub_k1.astype(BF16), sub_k2.astype(BF16), jnp.asarray(pos), jnp.asarray(valid))


def _gelu_tanh(x):
    return 0.5 * x * (1.0 + jnp.tanh(math.sqrt(2.0 / math.pi) * (x + 0.044715 * (x * x * x))))


def _pexp_body(x_ref, gf_ref, gl_ref, u_ref, vt_ref, w1_ref, n1_ref, w2_ref, r2_ref,
               y_ref, xb_sc, wt_sc, acc_sc, *, rows_per_step, tchunk):
    e = pl.program_id(1)
    tt = x_ref.shape[0]

    @pl.when(e == 0)
    def _():
        xb_sc[...] = _rms(x_ref[...], gf_ref[...], NORM_EPS).astype(BF16)
        acc_sc[...] = jnp.zeros(acc_sc.shape, F32)

    act = _dot_nt(u_ref[...], xb_sc[...])
    for rl in range(rows_per_step):
        r = e * rows_per_step + rl
        for c in range(tt // tchunk):
            cs = slice(c * tchunk, (c + 1) * tchunk)
            gate = jnp.zeros((N_KEYS, tchunk), F32)
            for h in range(PEER_HEADS):
                w1 = w1_ref[h, pl.ds(r, 1), cs]
                n1 = n1_ref[h, pl.ds(r, 1), cs]
                gate = gate + w1 * jnp.where(r2_ref[h, :, cs] < n1, w2_ref[h, :, cs], 0.0)
            a = act[rl * N_KEYS:(rl + 1) * N_KEYS, cs]
            wt_sc[rl * N_KEYS:(rl + 1) * N_KEYS, cs] = (gate * _gelu_tanh(a)).astype(BF16)
    acc_sc[...] += jnp.dot(vt_ref[...], wt_sc[...], preferred_element_type=F32)

    @pl.when(e == pl.num_programs(1) - 1)
    def _():
        x3 = x_ref[...] + acc_sc[...].T
        y_ref[...] = _rms(x3, gl_ref[...], NORM_EPS)


def _peer_experts(x, g_ffn, g_final, u_bf, vt_bf, tabs, tt=512, rows_per_step=4):
    m, d = x.shape
    tt = min(tt, m)
    n_exp = u_bf.shape[0]
    ne = rows_per_step * N_KEYS
    tab_spec = pl.BlockSpec((PEER_HEADS, N_KEYS, tt), lambda i, e: (0, 0, i))
    body = functools.partial(_pexp_body, rows_per_step=rows_per_step, tchunk=min(256, tt))
    return pl.pallas_call(
        body,
        grid=(m // tt, n_exp // ne),
        in_specs=[
            pl.BlockSpec((tt, d), lambda i, e: (i, 0)),
            pl.BlockSpec((1, d), lambda i, e: (0, 0)),
            pl.BlockSpec((1, d), lambda i, e: (0, 0)),
            pl.BlockSpec((ne, d), lambda i, e: (e, 0)),
            pl.BlockSpec((d, ne), lambda i, e: (0, e)),
            tab_spec, tab_spec, tab_spec, tab_spec,
        ],
        out_specs=pl.BlockSpec((tt, d), lambda i, e: (i, 0)),
        out_shape=jax.ShapeDtypeStruct((m, d), F32),
        scratch_shapes=[
            pltpu.VMEM((tt, d), BF16),
            pltpu.VMEM((ne, tt), BF16),
            pltpu.VMEM((d, tt), F32),
        ],
        compiler_params=_cparams(("parallel", "arbitrary")),
        name="peer_experts",
    )(x, g_ffn.reshape(1, d), g_final.reshape(1, d), u_bf, vt_bf, *tabs)


def _alibi_slopes(n):
    return jnp.asarray(2.0 ** (-8.0 * np.arange(1, n + 1) / n), dtype=F32)


def _project(x, g, w_in_bf, w_fl_bf, b_forget):
    outs = [_mm([x], [w_in_bf[:, k * 1024:(k + 1) * 1024]], gain=g) for k in range(6)]
    bias = jnp.zeros((LANES,), F32).at[:FX_HEADS].set(b_forget)
    logf = _mm([x], [w_fl_bf], gain=g, bias=bias)[:, :FX_HEADS]
    return outs, logf


def _tail(x, od, of, mem_k, mem_v, xq_tile, w, u_bf, vt_bf):
    x1 = _mm([od, of], [w["out_d"], w["out_f"]], res=x)
    q = _mm([x1], [w["mem_q"]], gain=w["norm_mem"])
    o = _cross_attn(q, mem_k, mem_v, xq_tile)
    x2 = _mm([o], [w["mem_o"]], res=x1)
    pq = _mm([x2], [w["peer_q"]], gain=w["norm_ffn"])
    tabs = _peer_select(pq, w["sub_k1"], w["sub_k2"])
    return _peer_experts(x2, w["norm_ffn"], w["norm_final"], u_bf, vt_bf, tabs)


def kernel(x_prompt, x_sample, mem_prompt, cache_diff_k, cache_diff_v, cache_fox_k, cache_fox_v,
           cache_fox_logf, cache_mem_k, cache_mem_v, page_table, norm_attn, w_in, b_forget,
           lambda_q1, lambda_k1, lambda_q2, lambda_k2, norm_sub, w_out, norm_mem, w_mem_q,
           w_mem_k, w_mem_v, w_mem_o, norm_ffn, w_peer_q, peer_sub_k1, peer_sub_k2, peer_u,
           peer_v, norm_final):
    depth = w_in.shape[0]
    assert depth == 1, "the residual chain below is written for a single layer"
    l = 0
    batch, seq, d = x_prompt.shape
    n_seq, dec, _ = x_sample.shape
    mem_len = mem_prompt.shape[1]
    lam_init = 0.8 - 0.6 * math.exp(-0.3 * l)
    slopes = _alibi_slopes(DA_HEADS)
    lam_vec = jnp.stack([lambda_q1[l], lambda_k1[l], lambda_q2[l], lambda_k2[l]])

    n_main = 6 * 1024
    w_in_bf = w_in[l, :, :n_main].astype(BF16)
    w_fl_bf = jnp.pad(w_in[l, :, n_main:], ((0, 0), (0, LANES - FX_HEADS))).astype(BF16)
    w = dict(
        out_d=w_out[l, :DA_WIDTH].astype(BF16), out_f=w_out[l, DA_WIDTH:].astype(BF16),
        mem_q=w_mem_q[l].astype(BF16), mem_o=w_mem_o[l].astype(BF16),
        peer_q=w_peer_q[l].astype(BF16), norm_mem=norm_mem[l], norm_ffn=norm_ffn[l],
        norm_final=norm_final, sub_k1=peer_sub_k1[l], sub_k2=peer_sub_k2[l])
    u_bf = peer_u[l].astype(BF16)
    vt_bf = peer_v[l].T.astype(BF16)

    xp = x_prompt.reshape(batch * seq, d)
    xs = x_sample.reshape(n_seq * dec, d)

    (pdq, pdk, pdv, pfq, pfk, pfv), plf = _project(xp, norm_attn[l], w_in_bf, w_fl_bf, b_forget[l])
    cum = _cumsum_lanes(jnp.transpose(plf.reshape(batch, seq, FX_HEADS), (0, 2, 1)))
    od = _prompt_diff(pdq, pdk, pdv, slopes, lam_vec, norm_sub[l], batch, seq, lam_init)
    of = _prompt_fox(pfq, pfk, pfv, cum, batch, seq)
    mem2 = mem_prompt.reshape(batch * mem_len, d)
    pmk = _mm([mem2], [w_mem_k[l].astype(BF16)])
    pmv = _mm([mem2], [w_mem_v[l].astype(BF16)])
    y_prompt = _tail(xp, od, of, pmk.reshape(batch, mem_len, d), pmv.reshape(batch, mem_len, d),
                     512, w, u_bf, vt_bf)

    (sdq, sdk, sdv, sfq, sfk, sfv), slf = _project(xs, norm_attn[l], w_in_bf, w_fl_bf, b_forget[l])
    sod = _sample_diff(sdq, sdk, sdv, cache_diff_k[l], cache_diff_v[l], page_table, slopes,
                       lam_vec, norm_sub[l], lam_init, dec)
    pool_lf_t = jnp.transpose(cache_fox_logf[l], (0, 2, 1))
    sof = _sample_fox(sfq, sfk, sfv, slf, cache_fox_k[l], cache_fox_v[l], pool_lf_t,
                      page_table, dec)
    y_sample = _tail(xs, sod, sof, cache_mem_k[l].reshape(n_seq, mem_len, d),
                     cache_mem_v[l].reshape(n_seq, mem_len, d), dec, w, u_bf, vt_bf)

    return (y_prompt.reshape(batch, seq, d), y_sample.reshape(n_seq, dec, d),
            pdk.reshape(1, batch, seq, DA_HEADS, DA_V_DIM),
            pdv.reshape(1, batch, seq, DA_HEADS, DA_V_DIM),
            pfk.reshape(1, batch, seq, FX_HEADS, FX_DIM),
            pfv.reshape(1, batch, seq, FX_HEADS, FX_DIM),
            plf.reshape(1, batch, seq, FX_HEADS),
            pmk.reshape(1, batch, mem_len, MEM_HEADS, MEM_DIM),
            pmv.reshape(1, batch, mem_len, MEM_HEADS, MEM_DIM),
            sdk.reshape(1, n_seq, dec, DA_HEADS, DA_V_DIM),
            sdv.reshape(1, n_seq, dec, DA_HEADS, DA_V_DIM),
            sfk.reshape(1, n_seq, dec, FX_HEADS, FX_DIM),
            sfv.reshape(1, n_seq, dec, FX_HEADS, FX_DIM),
            slf.reshape(1, n_seq, dec, FX_HEADS))
```

```python
import functools
import math

import numpy as np
import jax
import jax.numpy as jnp
from jax import lax
from jax.experimental import pallas as pl
from jax.experimental.pallas import tpu as pltpu

F32 = jnp.float32
BF16 = jnp.bfloat16

D_MODEL = 2048
PAGE_SIZE = 128
DA_QK_DIM = 128
DA_V_DIM = 256
DA_HEADS = 4
DA_WIDTH = DA_HEADS * DA_V_DIM
FX_DIM = 128
FX_HEADS = 8
FX_WIDTH = FX_HEADS * FX_DIM
MEM_HEADS = 4
MEM_DIM = D_MODEL // MEM_HEADS
PEER_HEADS = 8
PEER_TOPK = 16
N_KEYS = 128
PEER_HALF = 128
NORM_EPS = 1e-6
SUBLN_EPS = 1e-5
NEG_BIG = -1e30

LANES = 128
VMEM_LIMIT = 56 * 1024 * 1024


def _cparams(sem):
    return pltpu.CompilerParams(dimension_semantics=sem, vmem_limit_bytes=VMEM_LIMIT)


def _dot_nt(a, b):
    return lax.dot_general(a, b, (((1,), (1,)), ((), ())), preferred_element_type=F32)


def _rms(x, g, eps):
    return x * lax.rsqrt(jnp.mean(x * x, axis=-1, keepdims=True) + eps) * g


def _resident(shape):
    return pl.BlockSpec(shape, lambda i: (0,) * len(shape), pipeline_mode=pl.Buffered(1))


def _log_sigmoid(z):
    return jnp.minimum(z, 0.0) - jnp.log1p(jnp.exp(-jnp.abs(z)))


def _proj_body(x_ref, g_ref, w_ref, wl_ref, b_ref, *o_refs, eps):
    xb = _rms(x_ref[...], g_ref[...], eps).astype(BF16)
    width = o_refs[0].shape[1]
    for k, o_ref in enumerate(o_refs[:-1]):
        o_ref[...] = jnp.dot(xb, w_ref[:, k * width:(k + 1) * width], preferred_element_type=F32)
    z = jnp.dot(xb, wl_ref[...], preferred_element_type=F32) + b_ref[...]
    o_refs[-1][...] = _log_sigmoid(z)


def _project(x, g, w_bf, wl_bf, bias, n_groups, tm=256):
    m, d = x.shape
    tm = min(tm, m)
    width = w_bf.shape[1] // n_groups
    nl = wl_bf.shape[1]
    row = lambda i: (i, 0)
    outs = pl.pallas_call(
        functools.partial(_proj_body, eps=NORM_EPS),
        grid=(m // tm,),
        in_specs=[pl.BlockSpec((tm, d), row), _resident((1, d)), _resident(w_bf.shape),
                  _resident(wl_bf.shape), _resident((1, nl))],
        out_specs=[pl.BlockSpec((tm, width), row)] * n_groups + [pl.BlockSpec((tm, nl), row)],
        out_shape=[jax.ShapeDtypeStruct((m, width), F32)] * n_groups
                  + [jax.ShapeDtypeStruct((m, nl), F32)],
        compiler_params=_cparams(("parallel",)),
        name="in_proj",
    )(x, g.reshape(1, d), w_bf, wl_bf, bias.reshape(1, nl))
    return outs[:-1], outs[-1]


def _mm_body(*refs, n_in, has_res, emit_norm, eps):
    xs = refs[:n_in]
    ws = refs[n_in:2 * n_in]
    pos = 2 * n_in
    r_ref = g_ref = None
    if has_res:
        r_ref = refs[pos]; pos += 1
    if emit_norm:
        g_ref = refs[pos]; pos += 1
    o_ref = refs[pos]
    acc = None
    for x_ref, w_ref in zip(xs, ws):
        d = jnp.dot(x_ref[...].astype(BF16), w_ref[...], preferred_element_type=F32)
        acc = d if acc is None else acc + d
    if has_res:
        acc = acc + r_ref[...]
    o_ref[...] = acc
    if emit_norm:
        refs[pos + 1][...] = _rms(acc, g_ref[...], eps).astype(BF16)


def _mm(xs, ws, *, res=None, norm_gain=None, eps=NORM_EPS, tm=512):
    m = xs[0].shape[0]
    n = ws[0].shape[1]
    tm = min(tm, m)
    assert m % tm == 0
    row = lambda i: (i, 0)
    in_specs = [pl.BlockSpec((tm, x.shape[1]), row) for x in xs]
    in_specs += [_resident(w.shape) for w in ws]
    args = list(xs) + list(ws)
    if res is not None:
        in_specs.append(pl.BlockSpec((tm, n), row))
        args.append(res)
    out_specs = [pl.BlockSpec((tm, n), row)]
    out_shape = [jax.ShapeDtypeStruct((m, n), F32)]
    if norm_gain is not None:
        in_specs.append(_resident((1, n)))
        args.append(norm_gain.reshape(1, n))
        out_specs.append(pl.BlockSpec((tm, n), row))
        out_shape.append(jax.ShapeDtypeStruct((m, n), BF16))
    body = functools.partial(_mm_body, n_in=len(xs), has_res=res is not None,
                             emit_norm=norm_gain is not None, eps=eps)
    outs = pl.pallas_call(
        body, grid=(m // tm,), in_specs=in_specs, out_specs=out_specs, out_shape=out_shape,
        compiler_params=_cparams(("parallel",)), name="token_matmul",
    )(*args)
    return outs if norm_gain is not None else outs[0]


def _lambda(lv, lam_init):
    s1 = jnp.sum(lv[0:1] * lv[1:2], axis=-1, keepdims=True)
    s2 = jnp.sum(lv[2:3] * lv[3:4], axis=-1, keepdims=True)
    return jnp.exp(s1) - jnp.exp(s2) + lam_init


def _online(s, m_old, l_old):
    m_new = jnp.maximum(m_old, jnp.max(s, axis=-1, keepdims=True))
    p = jnp.exp(s - m_new)
    alpha = jnp.exp(m_old - m_new)
    l_new = alpha * l_old + jnp.sum(p, axis=-1, keepdims=True)
    return p, alpha, m_new, l_new


def _col_to_row(col):
    n = col.shape[0]
    eye = lax.broadcasted_iota(jnp.int32, (n, n), 0) == lax.broadcasted_iota(jnp.int32, (n, n), 1)
    return jnp.sum(jnp.where(eye, col, 0.0), axis=0, keepdims=True)


def _row_to_col(row):
    n = row.shape[1]
    eye = lax.broadcasted_iota(jnp.int32, (n, n), 0) == lax.broadcasted_iota(jnp.int32, (n, n), 1)
    return jnp.sum(jnp.where(eye, row, 0.0), axis=1, keepdims=True)


def _diff_finish(a1, l1, a2, l2, lam, g, lam_init):
    o = a1 / l1 - lam * (a2 / l2)
    return _rms(o, g, SUBLN_EPS) * (1.0 - lam_init)


PDIFF_HEADS_PER_STEP = 2
PFOX_HEADS_PER_STEP = 4


def _pdiff_body(slope_ref, lam_ref, q_ref, k_ref, v_ref, g_ref, o_ref, *, tq, lam_init, hb):
    hg = pl.program_id(1)
    qi = pl.program_id(2)
    lam = _lambda(lam_ref[...], lam_init)
    scale = DA_QK_DIM ** -0.5
    nmap = 2 * hb
    qs = [(q_ref[:, i * DA_QK_DIM:(i + 1) * DA_QK_DIM] * scale).astype(BF16) for i in range(nmap)]
    slopes = [slope_ref[hg * hb + hh] for hh in range(hb)]
    rc = (lax.broadcasted_iota(jnp.int32, (tq, tq), 0)
          - lax.broadcasted_iota(jnp.int32, (tq, tq), 1))

    def body(j, carry):
        start = pl.multiple_of(j * tq, tq)
        dist = rc + (qi - j) * tq
        mask = dist >= 0
        distf = dist.astype(F32)
        out = []
        for hh in range(hb):
            bias = -slopes[hh] * distf
            v = v_ref[pl.ds(start, tq), hh * DA_V_DIM:(hh + 1) * DA_V_DIM].astype(BF16)
            for mp in range(2):
                i = 2 * hh + mp
                m, l, a = carry[i]
                k = k_ref[pl.ds(start, tq), i * DA_QK_DIM:(i + 1) * DA_QK_DIM].astype(BF16)
                s = jnp.where(mask, _dot_nt(qs[i], k) + bias, -jnp.inf)
                p, al, m, l = _online(s, m, l)
                a = al * a + jnp.dot(p.astype(BF16), v, preferred_element_type=F32)
                out.append((m, l, a))
        return tuple(out)

    init = (jnp.full((tq, 1), NEG_BIG, F32), jnp.zeros((tq, 1), F32), jnp.zeros((tq, DA_V_DIM), F32))
    res = lax.fori_loop(0, qi + 1, body, (init,) * nmap)
    for hh in range(hb):
        (_, l1, a1), (_, l2, a2) = res[2 * hh], res[2 * hh + 1]
        o_ref[:, hh * DA_V_DIM:(hh + 1) * DA_V_DIM] = _diff_finish(a1, l1, a2, l2, lam, g_ref[hh], lam_init)


def _prompt_diff(dq, dk, dv, slopes, lam_vec, g_sub, batch, seq, lam_init, tq=256,
                 hb=PDIFF_HEADS_PER_STEP):
    nq = seq // tq
    w = hb * DA_V_DIM
    body = functools.partial(_pdiff_body, tq=tq, lam_init=lam_init, hb=hb)
    return pl.pallas_call(
        body,
        grid=(batch, DA_HEADS // hb, nq),
        in_specs=[
            pl.BlockSpec(memory_space=pltpu.SMEM),
            pl.BlockSpec((4, DA_QK_DIM), lambda b, h, i: (0, 0)),
            pl.BlockSpec((tq, w), lambda b, h, i: (b * nq + i, h)),
            pl.BlockSpec((seq, w), lambda b, h, i: (b, h)),
            pl.BlockSpec((seq, w), lambda b, h, i: (b, h)),
            pl.BlockSpec((hb, 1, DA_V_DIM), lambda b, h, i: (h, 0, 0)),
        ],
        out_specs=pl.BlockSpec((tq, w), lambda b, h, i: (b * nq + i, h)),
        out_shape=jax.ShapeDtypeStruct((batch * seq, DA_WIDTH), F32),
        compiler_params=_cparams(("parallel", "parallel", "arbitrary")),
        name="prompt_diff",
    )(slopes, lam_vec, dq, dk, dv, g_sub.reshape(DA_HEADS, 1, DA_V_DIM))


def _cumsum_body(x_ref, o_ref):
    x = x_ref[0]
    n = x.shape[1]
    lane = lax.broadcasted_iota(jnp.int32, x.shape, 1)
    s = 1
    while s < n:
        x = x + jnp.where(lane >= s, pltpu.roll(x, s, axis=1), 0.0)
        s *= 2
    o_ref[0] = x


def _cumsum_lanes(x):
    b, h, n = x.shape
    return pl.pallas_call(
        _cumsum_body,
        grid=(b,),
        in_specs=[pl.BlockSpec((1, h, n), lambda i: (i, 0, 0))],
        out_specs=pl.BlockSpec((1, h, n), lambda i: (i, 0, 0)),
        out_shape=jax.ShapeDtypeStruct((b, h, n), F32),
        compiler_params=_cparams(("parallel",)),
        name="logf_cumsum",
    )(x)


def _pfox_body(q_ref, k_ref, v_ref, c_ref, o_ref, *, tq, hb):
    qi = pl.program_id(2)
    scale = FX_DIM ** -0.5
    qs = [(q_ref[:, hh * FX_DIM:(hh + 1) * FX_DIM] * scale).astype(BF16) for hh in range(hb)]
    rc = (lax.broadcasted_iota(jnp.int32, (tq, tq), 0)
          - lax.broadcasted_iota(jnp.int32, (tq, tq), 1))
    qstart = pl.multiple_of(qi * tq, tq)
    cqs = [_row_to_col(c_ref[0, hh, :, pl.ds(qstart, tq)]) for hh in range(hb)]

    def body(j, carry):
        start = pl.multiple_of(j * tq, tq)
        mask = rc + (qi - j) * tq >= 0
        out = []
        for hh in range(hb):
            m, l, a = carry[hh]
            k = k_ref[pl.ds(start, tq), hh * FX_DIM:(hh + 1) * FX_DIM].astype(BF16)
            v = v_ref[pl.ds(start, tq), hh * FX_DIM:(hh + 1) * FX_DIM].astype(BF16)
            ck = c_ref[0, hh, :, pl.ds(start, tq)]
            s = jnp.where(mask, _dot_nt(qs[hh], k) + cqs[hh] - ck, -jnp.inf)
            p, al, m, l = _online(s, m, l)
            a = al * a + jnp.dot(p.astype(BF16), v, preferred_element_type=F32)
            out.append((m, l, a))
        return tuple(out)

    init = (jnp.full((tq, 1), NEG_BIG, F32), jnp.zeros((tq, 1), F32), jnp.zeros((tq, FX_DIM), F32))
    res = lax.fori_loop(0, qi + 1, body, (init,) * hb)
    for hh in range(hb):
        _, l, a = res[hh]
        o_ref[:, hh * FX_DIM:(hh + 1) * FX_DIM] = a / l


def _prompt_fox(fq, fk, fv, cum, batch, seq, tq=256, hb=PFOX_HEADS_PER_STEP):
    nq = seq // tq
    w = hb * FX_DIM
    return pl.pallas_call(
        functools.partial(_pfox_body, tq=tq, hb=hb),
        grid=(batch, FX_HEADS // hb, nq),
        in_specs=[
            pl.BlockSpec((tq, w), lambda b, h, i: (b * nq + i, h)),
            pl.BlockSpec((seq, w), lambda b, h, i: (b, h)),
            pl.BlockSpec((seq, w), lambda b, h, i: (b, h)),
            pl.BlockSpec((1, hb, 1, seq), lambda b, h, i: (b, h, 0, 0)),
        ],
        out_specs=pl.BlockSpec((tq, w), lambda b, h, i: (b * nq + i, h)),
        out_shape=jax.ShapeDtypeStruct((batch * seq, FX_WIDTH), F32),
        compiler_params=_cparams(("parallel", "parallel", "arbitrary")),
        name="prompt_fox",
    )(fq, fk, fv, cum.reshape(batch, FX_HEADS, 1, seq))


PAGES_PER_STEP = 8
ROWS_PER_KEY = 8


def _page_rows(ref, sub):
    return ref[pl.ds(0, 1), pl.ds(sub, PAGE_SIZE, stride=ROWS_PER_KEY), :][0]


def _pad_keys(x):
    return jnp.concatenate([x, jnp.zeros((PAGE_SIZE - x.shape[0], x.shape[1]), x.dtype)], axis=0)


def _attend(q_blocks, k_blocks, v_blocks, bias, mask, m_sc, l_sc, a_sc):
    dec = q_blocks[0].shape[0]
    s = jnp.concatenate(
        [jnp.concatenate([_dot_nt(q, k) for k in ks], axis=1) for q, ks in zip(q_blocks, k_blocks)],
        axis=0) + bias
    if mask is not None:
        s = jnp.where(mask, s, -jnp.inf)
    m_old = m_sc[...]
    m_new = jnp.maximum(m_old, jnp.max(s, axis=1, keepdims=True))
    p = jnp.exp(s - m_new)
    alpha = jnp.exp(m_old - m_new)
    m_sc[...] = m_new
    l_sc[...] = alpha * l_sc[...] + jnp.sum(p, axis=1, keepdims=True)
    r = dec * (len(q_blocks) // len(v_blocks))
    outs = []
    for j, vs in enumerate(v_blocks):
        o = None
        for g, v in enumerate(vs):
            pj = p[j * r:(j + 1) * r, g * PAGE_SIZE:(g + 1) * PAGE_SIZE].astype(BF16)
            d = jnp.dot(pj, v, preferred_element_type=F32)
            o = d if o is None else o + d
        outs.append(o)
    a_sc[...] = alpha * a_sc[...] + jnp.concatenate(outs, axis=0)


def _sdiff_body(pt_ref, slope_ref, lam_ref, q_ref, kn_ref, vn_ref, g_ref, *rest,
                dec, past, lam_init, npg):
    kc, vc = rest[:npg], rest[npg:2 * npg]
    o_ref = rest[2 * npg]
    m_sc, l_sc, a_sc = rest[2 * npg + 1:]
    p = pl.program_id(1)
    nb = 2 * DA_HEADS
    scale = DA_QK_DIM ** -0.5
    qb = [(q_ref[:, i * DA_QK_DIM:(i + 1) * DA_QK_DIM] * scale).astype(BF16) for i in range(nb)]
    slope = slope_ref[...]

    def alibi(width, base):
        row = lax.broadcasted_iota(jnp.int32, (nb * dec, width), 0)
        col = lax.broadcasted_iota(jnp.int32, (nb * dec, width), 1)
        dist = base + lax.rem(row, dec) - col
        return dist, -slope * dist.astype(F32)

    @pl.when(p == 0)
    def _():
        m_sc[...] = jnp.full(m_sc.shape, NEG_BIG, F32)
        l_sc[...] = jnp.zeros(l_sc.shape, F32)
        a_sc[...] = jnp.zeros(a_sc.shape, F32)
        kb = [[_pad_keys(kn_ref[:, i * DA_QK_DIM:(i + 1) * DA_QK_DIM]).astype(BF16)] for i in range(nb)]
        vb = [[_pad_keys(vn_ref[:, h * DA_V_DIM:(h + 1) * DA_V_DIM]).astype(BF16)]
              for h in range(DA_HEADS)]
        dist, bias = alibi(PAGE_SIZE, 0)
        _attend(qb, kb, vb, bias, dist >= 0, m_sc, l_sc, a_sc)

    kb = [[_page_rows(kc[g], mp * DA_HEADS + h).astype(BF16) for g in range(npg)]
          for h in range(DA_HEADS) for mp in range(2)]
    vb = [[jnp.concatenate([_page_rows(vc[g], h), _page_rows(vc[g], DA_HEADS + h)],
                           axis=1).astype(BF16) for g in range(npg)]
          for h in range(DA_HEADS)]
    _, bias = alibi(npg * PAGE_SIZE, past - p * (npg * PAGE_SIZE))
    _attend(qb, kb, vb, bias, None, m_sc, l_sc, a_sc)

    @pl.when(p == pl.num_programs(1) - 1)
    def _():
        lam = _lambda(lam_ref[...], lam_init)
        for h in range(DA_HEADS):
            r1 = slice((2 * h) * dec, (2 * h + 1) * dec)
            r2 = slice((2 * h + 1) * dec, (2 * h + 2) * dec)
            o = _diff_finish(a_sc[r1, :], l_sc[r1, :], a_sc[r2, :], l_sc[r2, :], lam, g_ref[h], lam_init)
            o_ref[:, h * DA_V_DIM:(h + 1) * DA_V_DIM] = o


def _stored_pages_diff(pool):
    n_phys = pool.shape[0]
    x = pool.reshape(n_phys, PAGE_SIZE, DA_HEADS, 2, LANES)
    return jnp.transpose(x, (0, 1, 3, 2, 4)).reshape(n_phys, PAGE_SIZE * ROWS_PER_KEY, LANES)


def _sample_diff(dq, dk_new, dv_new, pool_k, pool_v, page_table, slopes, lam_vec, g_sub,
                 lam_init, dec, npg=PAGES_PER_STEP):
    n_seq, n_pages = page_table.shape
    npg = math.gcd(npg, n_pages)
    nb = 2 * DA_HEADS
    body = functools.partial(_sdiff_body, dec=dec, past=n_pages * PAGE_SIZE, lam_init=lam_init,
                             npg=npg)
    tok = lambda s, p, pt: (s, 0)
    const2 = lambda s, p, pt: (0, 0)

    def page(g):
        return pl.BlockSpec((1, PAGE_SIZE * ROWS_PER_KEY, LANES),
                            lambda s, p, pt: (pt[s * n_pages + p * npg + g], 0, 0))

    slope_rows = jnp.repeat(slopes, 2 * dec).reshape(nb * dec, 1)
    grid_spec = pltpu.PrefetchScalarGridSpec(
        num_scalar_prefetch=1,
        grid=(n_seq, n_pages // npg),
        in_specs=[
            pl.BlockSpec((nb * dec, 1), const2),
            pl.BlockSpec((4, DA_QK_DIM), const2),
            pl.BlockSpec((dec, DA_WIDTH), tok),
            pl.BlockSpec((dec, DA_WIDTH), tok),
            pl.BlockSpec((dec, DA_WIDTH), tok),
            pl.BlockSpec((DA_HEADS, 1, DA_V_DIM), lambda s, p, pt: (0, 0, 0)),
        ] + [page(g) for g in range(npg)] * 2,
        out_specs=pl.BlockSpec((dec, DA_WIDTH), tok),
        scratch_shapes=[
            pltpu.VMEM((nb * dec, 1), F32),
            pltpu.VMEM((nb * dec, 1), F32),
            pltpu.VMEM((nb * dec, DA_V_DIM), F32),
        ],
    )
    pages_k = _stored_pages_diff(pool_k)
    pages_v = _stored_pages_diff(pool_v)
    return pl.pallas_call(
        body,
        grid_spec=grid_spec,
        out_shape=jax.ShapeDtypeStruct((n_seq * dec, DA_WIDTH), F32),
        compiler_params=_cparams(("parallel", "arbitrary")),
        name="sample_diff",
    )(page_table.reshape(-1), slope_rows, lam_vec, dq, dk_new, dv_new,
      g_sub.reshape(DA_HEADS, 1, DA_V_DIM), *([pages_k] * npg), *([pages_v] * npg))


def _bf16_pieces(x):
    hi = x.astype(BF16).astype(F32)
    r1 = x - hi
    mid = r1.astype(BF16).astype(F32)
    lo = (r1 - mid).astype(BF16).astype(F32)
    return hi, mid, lo


def _sfox_body(pt_ref, q_ref, kn_ref, vn_ref, lfn_ref, *rest, dec, npg):
    kc, vc, lfc = rest[:npg], rest[npg:2 * npg], rest[2 * npg:3 * npg]
    o_ref = rest[3 * npg]
    m_sc, l_sc, a_sc, cq_sc, carry_sc = rest[3 * npg + 1:]
    p = pl.program_id(1)
    scale = FX_DIM ** -0.5
    qb = [(q_ref[:, h * FX_DIM:(h + 1) * FX_DIM] * scale).astype(BF16) for h in range(FX_HEADS)]

    @pl.when(p == 0)
    def _():
        m_sc[...] = jnp.full(m_sc.shape, NEG_BIG, F32)
        l_sc[...] = jnp.zeros(l_sc.shape, F32)
        a_sc[...] = jnp.zeros(a_sc.shape, F32)
        carry_sc[...] = jnp.zeros(carry_sc.shape, F32)
        lfn = lfn_ref[...]
        rows = [lfn[0:1]]
        for t in range(1, dec):
            rows.append(rows[-1] + lfn[t:t + 1])
        r8 = lax.broadcasted_iota(jnp.int32, (dec, FX_HEADS), 0)
        cum = jnp.zeros((dec, FX_HEADS), F32)
        for t in range(dec):
            cum = jnp.where(r8 == t, rows[t], cum)
        cq_sc[...] = cum
        kb = [[_pad_keys(kn_ref[:, h * FX_DIM:(h + 1) * FX_DIM]).astype(BF16)] for h in range(FX_HEADS)]
        vb = [[_pad_keys(vn_ref[:, h * FX_DIM:(h + 1) * FX_DIM]).astype(BF16)] for h in range(FX_HEADS)]
        bias = []
        for h in range(FX_HEADS):
            cqh = cum[:, h:h + 1]
            ck = jnp.concatenate([_col_to_row(cqh), jnp.zeros((1, PAGE_SIZE - dec), F32)], axis=1)
            bias.append(cqh - ck)
        row = lax.broadcasted_iota(jnp.int32, (FX_HEADS * dec, PAGE_SIZE), 0)
        col = lax.broadcasted_iota(jnp.int32, (FX_HEADS * dec, PAGE_SIZE), 1)
        _attend(qb, kb, vb, jnp.concatenate(bias, axis=0), col <= lax.rem(row, dec), m_sc, l_sc, a_sc)

    ki = lax.broadcasted_iota(jnp.int32, (PAGE_SIZE, PAGE_SIZE), 0)
    kj = lax.broadcasted_iota(jnp.int32, (PAGE_SIZE, PAGE_SIZE), 1)
    later = jnp.where(ki > kj, 1.0, 0.0).astype(BF16)
    lf = [lfc[g][0] for g in range(npg)]
    pieces = [_bf16_pieces(x) for x in lf]
    lhs = jnp.concatenate([pc[i] for i in range(3) for pc in pieces], axis=0).astype(BF16)
    prod = jnp.dot(lhs, later, preferred_element_type=F32)
    nh = FX_HEADS
    carry = carry_sc[...]
    after = []
    for g in range(npg):
        inside = (prod[g * nh:(g + 1) * nh] + prod[(npg + g) * nh:(npg + g + 1) * nh]
                  + prod[(2 * npg + g) * nh:(2 * npg + g + 1) * nh])
        after.append(inside + carry)
        carry = carry + inside[:, 0:1] + lf[g][:, 0:1]
    carry_sc[...] = carry
    cq = cq_sc[...]
    bias = jnp.concatenate(
        [cq[:, h:h + 1] + jnp.concatenate([a[h:h + 1, :] for a in after], axis=1)
         for h in range(FX_HEADS)], axis=0)
    kb = [[_page_rows(kc[g], h).astype(BF16) for g in range(npg)] for h in range(FX_HEADS)]
    vb = [[_page_rows(vc[g], h).astype(BF16) for g in range(npg)] for h in range(FX_HEADS)]
    _attend(qb, kb, vb, bias, None, m_sc, l_sc, a_sc)

    @pl.when(p == pl.num_programs(1) - 1)
    def _():
        o = a_sc[...] / l_sc[...]
        for h in range(FX_HEADS):
            o_ref[:, h * FX_DIM:(h + 1) * FX_DIM] = o[h * dec:(h + 1) * dec, :]


def _sample_fox(fq, fk_new, fv_new, lf_new, pool_k, pool_v, pool_lf_t, page_table, dec,
                npg=PAGES_PER_STEP):
    n_seq, n_pages = page_table.shape
    npg = math.gcd(npg, n_pages)
    n_phys = pool_k.shape[0]
    pages_k = pool_k.reshape(n_phys, PAGE_SIZE * ROWS_PER_KEY, LANES)
    pages_v = pool_v.reshape(n_phys, PAGE_SIZE * ROWS_PER_KEY, LANES)
    tok = lambda s, p, pt: (s, 0)

    def page(g, shape):
        return pl.BlockSpec(shape, lambda s, p, pt: (pt[s * n_pages + n_pages - 1 - (p * npg + g)], 0, 0))

    grid_spec = pltpu.PrefetchScalarGridSpec(
        num_scalar_prefetch=1,
        grid=(n_seq, n_pages // npg),
        in_specs=[
            pl.BlockSpec((dec, FX_WIDTH), tok),
            pl.BlockSpec((dec, FX_WIDTH), tok),
            pl.BlockSpec((dec, FX_WIDTH), tok),
            pl.BlockSpec((dec, FX_HEADS), tok),
        ] + [page(g, (1, PAGE_SIZE * ROWS_PER_KEY, LANES)) for g in range(npg)] * 2
          + [page(g, (1, FX_HEADS, PAGE_SIZE)) for g in range(npg)],
        out_specs=pl.BlockSpec((dec, FX_WIDTH), tok),
        scratch_shapes=[
            pltpu.VMEM((FX_HEADS * dec, 1), F32),
            pltpu.VMEM((FX_HEADS * dec, 1), F32),
            pltpu.VMEM((FX_HEADS * dec, FX_DIM), F32),
            pltpu.VMEM((dec, FX_HEADS), F32),
            pltpu.VMEM((FX_HEADS, 1), F32),
        ],
    )
    return pl.pallas_call(
        functools.partial(_sfox_body, dec=dec, npg=npg),
        grid_spec=grid_spec,
        out_shape=jax.ShapeDtypeStruct((n_seq * dec, FX_WIDTH), F32),
        compiler_params=_cparams(("parallel", "arbitrary")),
        name="sample_fox",
    )(page_table.reshape(-1), fq, fk_new, fv_new, lf_new,
      *([pages_k] * npg), *([pages_v] * npg), *([pool_lf_t] * npg))


def _xattn_body(q_ref, k_ref, v_ref, o_ref):
    scale = MEM_DIM ** -0.5
    s = _dot_nt(q_ref[...].astype(BF16), k_ref[0].astype(BF16)) * scale
    s = s - jnp.max(s, axis=-1, keepdims=True)
    e = jnp.exp(s)
    p = e / jnp.sum(e, axis=-1, keepdims=True)
    o_ref[...] = jnp.dot(p.astype(BF16), v_ref[0].astype(BF16), preferred_element_type=F32)


def _cross_attn(q, mem_k, mem_v, tq):
    groups, mem_len, _ = mem_k.shape
    m = q.shape[0]
    nq = m // (groups * tq)
    return pl.pallas_call(
        _xattn_body,
        grid=(groups, nq, MEM_HEADS),
        in_specs=[
            pl.BlockSpec((tq, MEM_DIM), lambda g, i, h: (g * nq + i, h)),
            pl.BlockSpec((1, mem_len, MEM_DIM), lambda g, i, h: (g, 0, h)),
            pl.BlockSpec((1, mem_len, MEM_DIM), lambda g, i, h: (g, 0, h)),
        ],
        out_specs=pl.BlockSpec((tq, MEM_DIM), lambda g, i, h: (g * nq + i, h)),
        out_shape=jax.ShapeDtypeStruct((m, D_MODEL), F32),
        compiler_params=_cparams(("parallel", "parallel", "arbitrary")),
    )(q, mem_k, mem_v)


MEM_ROWS_PER_TOKEN = MEM_HEADS * MEM_DIM // LANES


def _xattn_stored_body(q_ref, k_ref, v_ref, o_ref, *, mem_len):
    scale = MEM_DIM ** -0.5
    groups = MEM_DIM // LANES

    def head_block(ref, h):
        return jnp.concatenate(
            [ref[pl.ds(0, 1), pl.ds(c * MEM_HEADS + h, mem_len, stride=MEM_ROWS_PER_TOKEN), :][0]
             for c in range(groups)], axis=1).astype(BF16)

    for h in range(MEM_HEADS):
        q = (q_ref[:, h * MEM_DIM:(h + 1) * MEM_DIM] * scale).astype(BF16)
        s = _dot_nt(q, head_block(k_ref, h))
        s = s - jnp.max(s, axis=-1, keepdims=True)
        e = jnp.exp(s)
        p = e / jnp.sum(e, axis=-1, keepdims=True)
        o_ref[:, h * MEM_DIM:(h + 1) * MEM_DIM] = jnp.dot(
            p.astype(BF16), head_block(v_ref, h), preferred_element_type=F32)


def _stored_mem(cache):
    n_seq, mem_len = cache.shape[:2]
    x = cache.reshape(n_seq, mem_len, MEM_HEADS, MEM_DIM // LANES, LANES)
    return jnp.transpose(x, (0, 1, 3, 2, 4)).reshape(n_seq, mem_len * MEM_ROWS_PER_TOKEN, LANES)


def _cross_attn_stored(q, cache_k, cache_v, dec):
    n_seq, mem_len = cache_k.shape[:2]
    rows = mem_len * MEM_ROWS_PER_TOKEN
    return pl.pallas_call(
        functools.partial(_xattn_stored_body, mem_len=mem_len),
        grid=(n_seq,),
        in_specs=[
            pl.BlockSpec((dec, D_MODEL), lambda s: (s, 0)),
            pl.BlockSpec((1, rows, LANES), lambda s: (s, 0, 0)),
            pl.BlockSpec((1, rows, LANES), lambda s: (s, 0, 0)),
        ],
        out_specs=pl.BlockSpec((dec, D_MODEL), lambda s: (s, 0)),
        out_shape=jax.ShapeDtypeStruct((n_seq * dec, D_MODEL), F32),
        compiler_params=_cparams(("parallel",)),
        name="sample_cross_attn",
    )(q, _stored_mem(cache_k), _stored_mem(cache_v))


def _cand_layout():
    pos, valid = [], []
    for b in range(16):
        pos.append(b); valid.append(True)
    for a in range(1, 8):
        for b in range(8):
            pos.append(a * 16 + b); valid.append((a + 1) * (b + 1) <= PEER_TOPK)
    for a in range(8, 16):
        pos.append(a * 16); valid.append(True)
    return np.asarray(pos, np.float32)[:, None], np.asarray(valid, np.float32)[:, None]


def _top16(s, iota_f):
    rank = jnp.full(s.shape, float(PEER_TOPK), F32)
    vals = []
    work = s
    for a in range(PEER_TOPK):
        m = jnp.max(work, axis=0, keepdims=True)
        idx = jnp.min(jnp.where(work == m, iota_f, 1e9), axis=0, keepdims=True)
        hit = iota_f == idx
        rank = jnp.where(hit, float(a), rank)
        work = jnp.where(hit, -jnp.inf, work)
        vals.append(m)
    return rank, vals


def _rows_to_array(rows, n):
    tt = rows[0].shape[1]
    r = lax.broadcasted_iota(jnp.int32, (n, tt), 0)
    out = jnp.zeros((n, tt), F32)
    for a in range(n):
        out = jnp.where(r == a, rows[a], out)
    return out


def _psel_body(q_ref, k1_ref, k2_ref, pos_ref, valid_ref, w1_ref, n1_ref, w2_ref, r2_ref):
    tt = q_ref.shape[0]
    iota_f = lax.broadcasted_iota(jnp.int32, (N_KEYS, tt), 0).astype(F32)
    posb = jnp.broadcast_to(pos_ref[...], (pos_ref.shape[0], tt))
    validb = jnp.broadcast_to(valid_ref[...], posb.shape) > 0.5
    i16 = lax.broadcasted_iota(jnp.int32, (PEER_TOPK, tt), 0).astype(F32)
    for h in range(PEER_HEADS):
        lo = h * 2 * PEER_HALF
        qa = q_ref[:, lo:lo + PEER_HALF].astype(BF16)
        qb = q_ref[:, lo + PEER_HALF:lo + 2 * PEER_HALF].astype(BF16)
        s1 = _dot_nt(k1_ref[...], qa)
        s2 = _dot_nt(k2_ref[...], qb)
        rank1, v1 = _top16(s1, iota_f)
        rank2, v2 = _top16(s2, iota_f)
        v1a = _rows_to_array(v1, PEER_TOPK)
        v2a = _rows_to_array(v2, PEER_TOPK)
        blocks = [v1[0] + v2a]
        for a in range(1, 8):
            blocks.append(v1[a] + v2a[0:8])
        blocks.append(v1a[8:16] + v2[0])
        work = jnp.where(validb, jnp.concatenate(blocks, axis=0), -jnp.inf)
        top = v1[0] + v2[0]
        z = jnp.zeros((1, tt), F32)
        cnt = jnp.zeros((PEER_TOPK, tt), F32)
        for _ in range(PEER_TOPK):
            m = jnp.max(work, axis=0, keepdims=True)
            pmin = jnp.min(jnp.where(work == m, posb, 1e9), axis=0, keepdims=True)
            work = jnp.where(posb == pmin, -jnp.inf, work)
            z = z + jnp.exp(m - top)
            cnt = cnt + jnp.where(i16 == jnp.floor(pmin * (1.0 / 16.0)), 1.0, 0.0)
        n1 = jnp.zeros((N_KEYS, tt), F32)
        for a in range(PEER_TOPK):
            n1 = jnp.where(rank1 == float(a), cnt[a:a + 1], n1)
        w1_ref[h] = jnp.where(rank1 < PEER_TOPK, jnp.exp(s1 - v1[0]), 0.0)
        n1_ref[h] = n1
        w2_ref[h] = jnp.where(rank2 < PEER_TOPK, jnp.exp(s2 - v2[0]) / z, 0.0).astype(BF16)
        r2_ref[h] = rank2.astype(BF16)


def _peer_select(q, sub_k1, sub_k2, tt=256):
    m = q.shape[0]
    tt = min(tt, m)
    pos, valid = _cand_layout()
    n_c = pos.shape[0]
    tab = jax.ShapeDtypeStruct((PEER_HEADS, N_KEYS, m), F32)
    tab16 = jax.ShapeDtypeStruct((PEER_HEADS, N_KEYS, m), BF16)
    tab_spec = pl.BlockSpec((PEER_HEADS, N_KEYS, tt), lambda i: (0, 0, i))
    return pl.pallas_call(
        _psel_body,
        grid=(m // tt,),
        in_specs=[
            pl.BlockSpec((tt, q.shape[1]), lambda i: (i, 0)),
            pl.BlockSpec((N_KEYS, PEER_HALF), lambda i: (0, 0)),
            pl.BlockSpec((N_KEYS, PEER_HALF), lambda i: (0, 0)),
            pl.BlockSpec((n_c, 1), lambda i: (0, 0)),
            pl.BlockSpec((n_c, 1), lambda i: (0, 0)),
        ],
        out_specs=[tab_spec] * 4,
        out_shape=[tab, tab, tab16, tab16],
        compiler_params=_cparams(("parallel",)),
        name="peer_select",
    )(q, sub_k1.astype(BF16), sub_k2.astype(BF16), jnp.asarray(pos), jnp.asarray(valid))


def _gelu_tanh(x):
    return 0.5 * x * (1.0 + jnp.tanh(math.sqrt(2.0 / math.pi) * (x + 0.044715 * (x * x * x))))


def _pexp_body(x_ref, xn_ref, gl_ref, u_ref, vt_ref, w1_ref, n1_ref, w2_ref, r2_ref,
               y_ref, wt_sc, acc_sc, *, rows_per_step, tchunk):
    e = pl.program_id(1)
    tt = x_ref.shape[0]

    @pl.when(e == 0)
    def _():
        acc_sc[...] = jnp.zeros(acc_sc.shape, F32)

    act = _dot_nt(u_ref[...], xn_ref[...])
    for rl in range(rows_per_step):
        r = e * rows_per_step + rl
        for c in range(tt // tchunk):
            cs = slice(c * tchunk, (c + 1) * tchunk)
            gate = jnp.zeros((N_KEYS // 16, 16, tchunk), BF16)
            for h in range(PEER_HEADS):
                w1 = jnp.broadcast_to(w1_ref[h, pl.ds(r, 1), cs], (16, tchunk)).astype(BF16)
                n1 = jnp.broadcast_to(n1_ref[h, pl.ds(r, 1), cs], (16, tchunk)).astype(BF16)
                r2 = r2_ref[h, :, cs].reshape(N_KEYS // 16, 16, tchunk)
                w2 = w2_ref[h, :, cs].reshape(N_KEYS // 16, 16, tchunk)
                gate = gate + w1[None] * jnp.where(r2 < n1[None], w2, jnp.zeros_like(w2))
            a = act[rl * N_KEYS:(rl + 1) * N_KEYS, cs]
            g32 = gate.reshape(N_KEYS, tchunk).astype(F32)
            wt_sc[rl * N_KEYS:(rl + 1) * N_KEYS, cs] = (g32 * _gelu_tanh(a)).astype(BF16)
    acc_sc[...] += jnp.dot(vt_ref[...], wt_sc[...], preferred_element_type=F32)

    @pl.when(e == pl.num_programs(1) - 1)
    def _():
        x3 = x_ref[...] + acc_sc[...].T
        y_ref[...] = _rms(x3, gl_ref[...], NORM_EPS)


def _peer_experts(x, xn, g_final, u_bf, vt_bf, tabs, tt=512, rows_per_step=4):
    m, d = x.shape
    tt = min(tt, m)
    n_exp = u_bf.shape[0]
    ne = rows_per_step * N_KEYS
    tab_spec = pl.BlockSpec((PEER_HEADS, N_KEYS, tt), lambda i, e: (0, 0, i))
    body = functools.partial(_pexp_body, rows_per_step=rows_per_step, tchunk=min(256, tt))
    return pl.pallas_call(
        body,
        grid=(m // tt, n_exp // ne),
        in_specs=[
            pl.BlockSpec((tt, d), lambda i, e: (i, 0)),
            pl.BlockSpec((tt, d), lambda i, e: (i, 0)),
            pl.BlockSpec((1, d), lambda i, e: (0, 0)),
            pl.BlockSpec((ne, d), lambda i, e: (e, 0)),
            pl.BlockSpec((d, ne), lambda i, e: (0, e)),
            tab_spec, tab_spec, tab_spec, tab_spec,
        ],
        out_specs=pl.BlockSpec((tt, d), lambda i, e: (i, 0)),
        out_shape=jax.ShapeDtypeStruct((m, d), F32),
        scratch_shapes=[
            pltpu.VMEM((ne, tt), BF16),
            pltpu.VMEM((d, tt), F32),
        ],
        compiler_params=_cparams(("parallel", "arbitrary")),
        name="peer_experts",
    )(x, xn, g_final.reshape(1, d), u_bf, vt_bf, *tabs)


def _alibi_slopes(n):
    return jnp.asarray(2.0 ** (-8.0 * np.arange(1, n + 1) / n), dtype=F32)


def _tail(x, od, of, cross_attn, w, u_bf, vt_bf):
    x1, xn1 = _mm([od, of], [w["out_d"], w["out_f"]], res=x, norm_gain=w["norm_mem"])
    q = _mm([xn1], [w["mem_q"]])
    o = cross_attn(q)
    x2, xn2 = _mm([o], [w["mem_o"]], res=x1, norm_gain=w["norm_ffn"])
    pq = _mm([xn2], [w["peer_q"]])
    tabs = _peer_select(pq, w["sub_k1"], w["sub_k2"])
    return _peer_experts(x2, xn2, w["norm_final"], u_bf, vt_bf, tabs)


def kernel(x_prompt, x_sample, mem_prompt, cache_diff_k, cache_diff_v, cache_fox_k, cache_fox_v,
           cache_fox_logf, cache_mem_k, cache_mem_v, page_table, norm_attn, w_in, b_forget,
           lambda_q1, lambda_k1, lambda_q2, lambda_k2, norm_sub, w_out, norm_mem, w_mem_q,
           w_mem_k, w_mem_v, w_mem_o, norm_ffn, w_peer_q, peer_sub_k1, peer_sub_k2, peer_u,
           peer_v, norm_final):
    depth = w_in.shape[0]
    assert depth == 1, "the residual chain below is written for a single layer"
    l = 0
    batch, seq, d = x_prompt.shape
    n_seq, dec, _ = x_sample.shape
    mem_len = mem_prompt.shape[1]
    lam_init = 0.8 - 0.6 * math.exp(-0.3 * l)
    slopes = _alibi_slopes(DA_HEADS)
    lam_vec = jnp.stack([lambda_q1[l], lambda_k1[l], lambda_q2[l], lambda_k2[l]])

    n_main = 6 * 1024
    w_in_bf = w_in[l, :, :n_main].astype(BF16)
    w_fl_bf = jnp.pad(w_in[l, :, n_main:], ((0, 0), (0, LANES - FX_HEADS))).astype(BF16)
    w = dict(
        out_d=w_out[l, :DA_WIDTH].astype(BF16), out_f=w_out[l, DA_WIDTH:].astype(BF16),
        mem_q=w_mem_q[l].astype(BF16), mem_o=w_mem_o[l].astype(BF16),
        peer_q=w_peer_q[l].astype(BF16), norm_mem=norm_mem[l], norm_ffn=norm_ffn[l],
        norm_final=norm_final, sub_k1=peer_sub_k1[l], sub_k2=peer_sub_k2[l])
    u_bf = peer_u[l].astype(BF16)
    vt_bf = peer_v[l].T.astype(BF16)

    xp = x_prompt.reshape(batch * seq, d)
    xs = x_sample.reshape(n_seq * dec, d)

    bias_f = jnp.zeros((LANES,), F32).at[:FX_HEADS].set(b_forget[l])
    (pdq, pdk, pdv, pfq, pfk, pfv), plf = _project(xp, norm_attn[l], w_in_bf, w_fl_bf, bias_f, 6)
    plf = plf[:, :FX_HEADS]
    cum = _cumsum_lanes(jnp.transpose(plf.reshape(batch, seq, FX_HEADS), (0, 2, 1)))
    od = _prompt_diff(pdq, pdk, pdv, slopes, lam_vec, norm_sub[l], batch, seq, lam_init)
    of = _prompt_fox(pfq, pfk, pfv, cum, batch, seq)
    mem2 = mem_prompt.reshape(batch * mem_len, d)
    pmk = _mm([mem2], [w_mem_k[l].astype(BF16)])
    pmv = _mm([mem2], [w_mem_v[l].astype(BF16)])
    prompt_xattn = lambda q: _cross_attn(q, pmk.reshape(batch, mem_len, d),
                                         pmv.reshape(batch, mem_len, d), 512)
    y_prompt = _tail(xp, od, of, prompt_xattn, w, u_bf, vt_bf)

    (sdq, sdk, sdv, sfq, sfk, sfv), slf = _project(xs, norm_attn[l], w_in_bf, w_fl_bf, bias_f, 6)
    slf = slf[:, :FX_HEADS]
    sod = _sample_diff(sdq, sdk, sdv, cache_diff_k[l], cache_diff_v[l], page_table, slopes,
                       lam_vec, norm_sub[l], lam_init, dec)
    pool_lf_t = jnp.transpose(cache_fox_logf[l], (0, 2, 1))
    sof = _sample_fox(sfq, sfk, sfv, slf, cache_fox_k[l], cache_fox_v[l], pool_lf_t,
                      page_table, dec)
    sample_xattn = lambda q: _cross_attn_stored(q, cache_mem_k[l], cache_mem_v[l], dec)
    y_sample = _tail(xs, sod, sof, sample_xattn, w, u_bf, vt_bf)

    return (y_prompt.reshape(batch, seq, d), y_sample.reshape(n_seq, dec, d),
            pdk.reshape(1, batch, seq, DA_HEADS, DA_V_DIM),
            pdv.reshape(1, batch, seq, DA_HEADS, DA_V_DIM),
            pfk.reshape(1, batch, seq, FX_HEADS, FX_DIM),
            pfv.reshape(1, batch, seq, FX_HEADS, FX_DIM),
            plf.reshape(1, batch, seq, FX_HEADS),
            pmk.reshape(1, batch, mem_len, MEM_HEADS, MEM_DIM),
            pmv.reshape(1, batch, mem_len, MEM_HEADS, MEM_DIM),
            sdk.reshape(1, n_seq, dec, DA_HEADS, DA_V_DIM),
            sdv.reshape(1, n_seq, dec, DA_HEADS, DA_V_DIM),
            sfk.reshape(1, n_seq, dec, FX_HEADS, FX_DIM),
            sfv.reshape(1, n_seq, dec, FX_HEADS, FX_DIM),
            slf.reshape(1, n_seq, dec, FX_HEADS))
```

```python
import functools
import math

import numpy as np
import jax
import jax.numpy as jnp
from jax import lax
from jax.experimental import pallas as pl
from jax.experimental.pallas import tpu as pltpu

F32 = jnp.float32
BF16 = jnp.bfloat16

D_MODEL = 2048
PAGE_SIZE = 128
DA_QK_DIM = 128
DA_V_DIM = 256
DA_HEADS = 4
DA_WIDTH = DA_HEADS * DA_V_DIM
FX_DIM = 128
FX_HEADS = 8
FX_WIDTH = FX_HEADS * FX_DIM
MEM_HEADS = 4
MEM_DIM = D_MODEL // MEM_HEADS
PEER_HEADS = 8
PEER_TOPK = 16
N_KEYS = 128
PEER_HALF = 128
NORM_EPS = 1e-6
SUBLN_EPS = 1e-5
NEG_BIG = -1e30

LANES = 128
VMEM_LIMIT = 56 * 1024 * 1024


def _cparams(sem):
    return pltpu.CompilerParams(dimension_semantics=sem, vmem_limit_bytes=VMEM_LIMIT)


def _dot_nt(a, b):
    return lax.dot_general(a, b, (((1,), (1,)), ((), ())), preferred_element_type=F32)


def _rms(x, g, eps):
    return x * lax.rsqrt(jnp.mean(x * x, axis=-1, keepdims=True) + eps) * g


def _resident(shape):
    return pl.BlockSpec(shape, lambda i: (0,) * len(shape), pipeline_mode=pl.Buffered(1))


def _log_sigmoid(z):
    return jnp.minimum(z, 0.0) - jnp.log1p(jnp.exp(-jnp.abs(z)))


def _proj_body(x_ref, g_ref, w_ref, wl_ref, b_ref, *o_refs, eps):
    xb = _rms(x_ref[...], g_ref[...], eps).astype(BF16)
    width = o_refs[0].shape[1]
    for k, o_ref in enumerate(o_refs[:-1]):
        o_ref[...] = jnp.dot(xb, w_ref[:, k * width:(k + 1) * width], preferred_element_type=F32)
    z = jnp.dot(xb, wl_ref[...], preferred_element_type=F32) + b_ref[...]
    o_refs[-1][...] = _log_sigmoid(z)


def _project(x, g, w_bf, wl_bf, bias, n_groups, tm=256):
    m, d = x.shape
    tm = min(tm, m)
    width = w_bf.shape[1] // n_groups
    nl = wl_bf.shape[1]
    row = lambda i: (i, 0)
    outs = pl.pallas_call(
        functools.partial(_proj_body, eps=NORM_EPS),
        grid=(m // tm,),
        in_specs=[pl.BlockSpec((tm, d), row), _resident((1, d)), _resident(w_bf.shape),
                  _resident(wl_bf.shape), _resident((1, nl))],
        out_specs=[pl.BlockSpec((tm, width), row)] * n_groups + [pl.BlockSpec((tm, nl), row)],
        out_shape=[jax.ShapeDtypeStruct((m, width), F32)] * n_groups
                  + [jax.ShapeDtypeStruct((m, nl), F32)],
        compiler_params=_cparams(("parallel",)),
        name="in_proj",
    )(x, g.reshape(1, d), w_bf, wl_bf, bias.reshape(1, nl))
    return outs[:-1], outs[-1]


def _mm_body(*refs, n_in, has_res, emit_norm, eps):
    xs = refs[:n_in]
    ws = refs[n_in:2 * n_in]
    pos = 2 * n_in
    r_ref = g_ref = None
    if has_res:
        r_ref = refs[pos]; pos += 1
    if emit_norm:
        g_ref = refs[pos]; pos += 1
    o_ref = refs[pos]
    acc = None
    for x_ref, w_ref in zip(xs, ws):
        d = jnp.dot(x_ref[...].astype(BF16), w_ref[...], preferred_element_type=F32)
        acc = d if acc is None else acc + d
    if has_res:
        acc = acc + r_ref[...]
    o_ref[...] = acc
    if emit_norm:
        refs[pos + 1][...] = _rms(acc, g_ref[...], eps).astype(BF16)


def _mm(xs, ws, *, res=None, norm_gain=None, eps=NORM_EPS, tm=512):
    m = xs[0].shape[0]
    n = ws[0].shape[1]
    tm = min(tm, m)
    assert m % tm == 0
    row = lambda i: (i, 0)
    in_specs = [pl.BlockSpec((tm, x.shape[1]), row) for x in xs]
    in_specs += [_resident(w.shape) for w in ws]
    args = list(xs) + list(ws)
    if res is not None:
        in_specs.append(pl.BlockSpec((tm, n), row))
        args.append(res)
    out_specs = [pl.BlockSpec((tm, n), row)]
    out_shape = [jax.ShapeDtypeStruct((m, n), F32)]
    if norm_gain is not None:
        in_specs.append(_resident((1, n)))
        args.append(norm_gain.reshape(1, n))
        out_specs.append(pl.BlockSpec((tm, n), row))
        out_shape.append(jax.ShapeDtypeStruct((m, n), BF16))
    body = functools.partial(_mm_body, n_in=len(xs), has_res=res is not None,
                             emit_norm=norm_gain is not None, eps=eps)
    outs = pl.pallas_call(
        body, grid=(m // tm,), in_specs=in_specs, out_specs=out_specs, out_shape=out_shape,
        compiler_params=_cparams(("parallel",)), name="token_matmul",
    )(*args)
    return outs if norm_gain is not None else outs[0]


def _lambda(lv, lam_init):
    s1 = jnp.sum(lv[0:1] * lv[1:2], axis=-1, keepdims=True)
    s2 = jnp.sum(lv[2:3] * lv[3:4], axis=-1, keepdims=True)
    return jnp.exp(s1) - jnp.exp(s2) + lam_init


def _online(s, m_old, l_old):
    m_new = jnp.maximum(m_old, jnp.max(s, axis=-1, keepdims=True))
    p = jnp.exp(s - m_new)
    alpha = jnp.exp(m_old - m_new)
    l_new = alpha * l_old + jnp.sum(p, axis=-1, keepdims=True)
    return p, alpha, m_new, l_new


def _col_to_row(col):
    n = col.shape[0]
    eye = lax.broadcasted_iota(jnp.int32, (n, n), 0) == lax.broadcasted_iota(jnp.int32, (n, n), 1)
    return jnp.sum(jnp.where(eye, col, 0.0), axis=0, keepdims=True)


def _row_to_col(row):
    n = row.shape[1]
    eye = lax.broadcasted_iota(jnp.int32, (n, n), 0) == lax.broadcasted_iota(jnp.int32, (n, n), 1)
    return jnp.sum(jnp.where(eye, row, 0.0), axis=1, keepdims=True)


def _diff_finish(a1, l1, a2, l2, lam, g, lam_init):
    o = a1 / l1 - lam * (a2 / l2)
    return _rms(o, g, SUBLN_EPS) * (1.0 - lam_init)


PDIFF_HEADS_PER_STEP = 2
PFOX_HEADS_PER_STEP = 4


def _pdiff_body(slope_ref, lam_ref, q_ref, k_ref, v_ref, g_ref, o_ref, *, tq, lam_init, hb):
    hg = pl.program_id(1)
    qi = pl.program_id(2)
    lam = _lambda(lam_ref[...], lam_init)
    scale = DA_QK_DIM ** -0.5
    nmap = 2 * hb
    qs = [(q_ref[:, i * DA_QK_DIM:(i + 1) * DA_QK_DIM] * scale).astype(BF16) for i in range(nmap)]
    slopes = [slope_ref[hg * hb + hh] for hh in range(hb)]
    rc = (lax.broadcasted_iota(jnp.int32, (tq, tq), 0)
          - lax.broadcasted_iota(jnp.int32, (tq, tq), 1))

    def body(j, carry):
        start = pl.multiple_of(j * tq, tq)
        dist = rc + (qi - j) * tq
        mask = dist >= 0
        distf = dist.astype(F32)
        out = []
        for hh in range(hb):
            bias = -slopes[hh] * distf
            v = v_ref[pl.ds(start, tq), hh * DA_V_DIM:(hh + 1) * DA_V_DIM].astype(BF16)
            for mp in range(2):
                i = 2 * hh + mp
                m, l, a = carry[i]
                k = k_ref[pl.ds(start, tq), i * DA_QK_DIM:(i + 1) * DA_QK_DIM].astype(BF16)
                s = jnp.where(mask, _dot_nt(qs[i], k) + bias, -jnp.inf)
                p, al, m, l = _online(s, m, l)
                a = al * a + jnp.dot(p.astype(BF16), v, preferred_element_type=F32)
                out.append((m, l, a))
        return tuple(out)

    init = (jnp.full((tq, 1), NEG_BIG, F32), jnp.zeros((tq, 1), F32), jnp.zeros((tq, DA_V_DIM), F32))
    res = lax.fori_loop(0, qi + 1, body, (init,) * nmap)
    for hh in range(hb):
        (_, l1, a1), (_, l2, a2) = res[2 * hh], res[2 * hh + 1]
        o_ref[:, hh * DA_V_DIM:(hh + 1) * DA_V_DIM] = _diff_finish(a1, l1, a2, l2, lam, g_ref[hh], lam_init)


def _prompt_diff(dq, dk, dv, slopes, lam_vec, g_sub, batch, seq, lam_init, tq=256,
                 hb=PDIFF_HEADS_PER_STEP):
    nq = seq // tq
    w = hb * DA_V_DIM
    body = functools.partial(_pdiff_body, tq=tq, lam_init=lam_init, hb=hb)
    return pl.pallas_call(
        body,
        grid=(batch, DA_HEADS // hb, nq),
        in_specs=[
            pl.BlockSpec(memory_space=pltpu.SMEM),
            pl.BlockSpec((4, DA_QK_DIM), lambda b, h, i: (0, 0)),
            pl.BlockSpec((tq, w), lambda b, h, i: (b * nq + i, h)),
            pl.BlockSpec((seq, w), lambda b, h, i: (b, h)),
            pl.BlockSpec((seq, w), lambda b, h, i: (b, h)),
            pl.BlockSpec((hb, 1, DA_V_DIM), lambda b, h, i: (h, 0, 0)),
        ],
        out_specs=pl.BlockSpec((tq, w), lambda b, h, i: (b * nq + i, h)),
        out_shape=jax.ShapeDtypeStruct((batch * seq, DA_WIDTH), F32),
        compiler_params=_cparams(("parallel", "parallel", "arbitrary")),
        name="prompt_diff",
    )(slopes, lam_vec, dq, dk, dv, g_sub.reshape(DA_HEADS, 1, DA_V_DIM))


def _cumsum_body(x_ref, o_ref):
    x = x_ref[0]
    n = x.shape[1]
    lane = lax.broadcasted_iota(jnp.int32, x.shape, 1)
    s = 1
    while s < n:
        x = x + jnp.where(lane >= s, pltpu.roll(x, s, axis=1), 0.0)
        s *= 2
    o_ref[0] = x


def _cumsum_lanes(x):
    b, h, n = x.shape
    return pl.pallas_call(
        _cumsum_body,
        grid=(b,),
        in_specs=[pl.BlockSpec((1, h, n), lambda i: (i, 0, 0))],
        out_specs=pl.BlockSpec((1, h, n), lambda i: (i, 0, 0)),
        out_shape=jax.ShapeDtypeStruct((b, h, n), F32),
        compiler_params=_cparams(("parallel",)),
        name="logf_cumsum",
    )(x)


def _pfox_body(q_ref, k_ref, v_ref, c_ref, o_ref, *, tq, hb):
    qi = pl.program_id(2)
    scale = FX_DIM ** -0.5
    qs = [(q_ref[:, hh * FX_DIM:(hh + 1) * FX_DIM] * scale).astype(BF16) for hh in range(hb)]
    rc = (lax.broadcasted_iota(jnp.int32, (tq, tq), 0)
          - lax.broadcasted_iota(jnp.int32, (tq, tq), 1))
    qstart = pl.multiple_of(qi * tq, tq)
    cqs = [_row_to_col(c_ref[0, hh, :, pl.ds(qstart, tq)]) for hh in range(hb)]

    def body(j, carry):
        start = pl.multiple_of(j * tq, tq)
        mask = rc + (qi - j) * tq >= 0
        out = []
        for hh in range(hb):
            m, l, a = carry[hh]
            k = k_ref[pl.ds(start, tq), hh * FX_DIM:(hh + 1) * FX_DIM].astype(BF16)
            v = v_ref[pl.ds(start, tq), hh * FX_DIM:(hh + 1) * FX_DIM].astype(BF16)
            ck = c_ref[0, hh, :, pl.ds(start, tq)]
            s = jnp.where(mask, _dot_nt(qs[hh], k) + cqs[hh] - ck, -jnp.inf)
            p, al, m, l = _online(s, m, l)
            a = al * a + jnp.dot(p.astype(BF16), v, preferred_element_type=F32)
            out.append((m, l, a))
        return tuple(out)

    init = (jnp.full((tq, 1), NEG_BIG, F32), jnp.zeros((tq, 1), F32), jnp.zeros((tq, FX_DIM), F32))
    res = lax.fori_loop(0, qi + 1, body, (init,) * hb)
    for hh in range(hb):
        _, l, a = res[hh]
        o_ref[:, hh * FX_DIM:(hh + 1) * FX_DIM] = a / l


def _prompt_fox(fq, fk, fv, cum, batch, seq, tq=256, hb=PFOX_HEADS_PER_STEP):
    nq = seq // tq
    w = hb * FX_DIM
    return pl.pallas_call(
        functools.partial(_pfox_body, tq=tq, hb=hb),
        grid=(batch, FX_HEADS // hb, nq),
        in_specs=[
            pl.BlockSpec((tq, w), lambda b, h, i: (b * nq + i, h)),
            pl.BlockSpec((seq, w), lambda b, h, i: (b, h)),
            pl.BlockSpec((seq, w), lambda b, h, i: (b, h)),
            pl.BlockSpec((1, hb, 1, seq), lambda b, h, i: (b, h, 0, 0)),
        ],
        out_specs=pl.BlockSpec((tq, w), lambda b, h, i: (b * nq + i, h)),
        out_shape=jax.ShapeDtypeStruct((batch * seq, FX_WIDTH), F32),
        compiler_params=_cparams(("parallel", "parallel", "arbitrary")),
        name="prompt_fox",
    )(fq, fk, fv, cum.reshape(batch, FX_HEADS, 1, seq))


PAGES_PER_STEP = 8
ROWS_PER_KEY = 8


def _page_rows(ref, sub):
    return ref[pl.ds(0, 1), pl.ds(sub, PAGE_SIZE, stride=ROWS_PER_KEY), :][0]


def _pad_keys(x):
    return jnp.concatenate([x, jnp.zeros((PAGE_SIZE - x.shape[0], x.shape[1]), x.dtype)], axis=0)


def _attend(q_blocks, k_blocks, v_blocks, bias, mask, m_sc, l_sc, a_sc):
    dec = q_blocks[0].shape[0]
    s = jnp.concatenate(
        [jnp.concatenate([_dot_nt(q, k) for k in ks], axis=1) for q, ks in zip(q_blocks, k_blocks)],
        axis=0) + bias
    if mask is not None:
        s = jnp.where(mask, s, -jnp.inf)
    m_old = m_sc[...]
    m_new = jnp.maximum(m_old, jnp.max(s, axis=1, keepdims=True))
    p = jnp.exp(s - m_new)
    alpha = jnp.exp(m_old - m_new)
    m_sc[...] = m_new
    l_sc[...] = alpha * l_sc[...] + jnp.sum(p, axis=1, keepdims=True)
    r = dec * (len(q_blocks) // len(v_blocks))
    outs = []
    for j, vs in enumerate(v_blocks):
        o = None
        for g, v in enumerate(vs):
            pj = p[j * r:(j + 1) * r, g * PAGE_SIZE:(g + 1) * PAGE_SIZE].astype(BF16)
            d = jnp.dot(pj, v, preferred_element_type=F32)
            o = d if o is None else o + d
        outs.append(o)
    a_sc[...] = alpha * a_sc[...] + jnp.concatenate(outs, axis=0)


def _sdiff_body(pt_ref, slope_ref, lam_ref, q_ref, kn_ref, vn_ref, g_ref, *rest,
                dec, past, lam_init, npg):
    kc, vc = rest[:npg], rest[npg:2 * npg]
    o_ref = rest[2 * npg]
    m_sc, l_sc, a_sc = rest[2 * npg + 1:]
    p = pl.program_id(1)
    nb = 2 * DA_HEADS
    scale = DA_QK_DIM ** -0.5
    qb = [(q_ref[:, i * DA_QK_DIM:(i + 1) * DA_QK_DIM] * scale).astype(BF16) for i in range(nb)]
    slope = slope_ref[...]

    def alibi(width, base):
        row = lax.broadcasted_iota(jnp.int32, (nb * dec, width), 0)
        col = lax.broadcasted_iota(jnp.int32, (nb * dec, width), 1)
        dist = base + lax.rem(row, dec) - col
        return dist, -slope * dist.astype(F32)

    @pl.when(p == 0)
    def _():
        m_sc[...] = jnp.full(m_sc.shape, NEG_BIG, F32)
        l_sc[...] = jnp.zeros(l_sc.shape, F32)
        a_sc[...] = jnp.zeros(a_sc.shape, F32)
        kb = [[_pad_keys(kn_ref[:, i * DA_QK_DIM:(i + 1) * DA_QK_DIM]).astype(BF16)] for i in range(nb)]
        vb = [[_pad_keys(vn_ref[:, h * DA_V_DIM:(h + 1) * DA_V_DIM]).astype(BF16)]
              for h in range(DA_HEADS)]
        dist, bias = alibi(PAGE_SIZE, 0)
        _attend(qb, kb, vb, bias, dist >= 0, m_sc, l_sc, a_sc)

    kb = [[_page_rows(kc[g], mp * DA_HEADS + h).astype(BF16) for g in range(npg)]
          for h in range(DA_HEADS) for mp in range(2)]
    vb = [[jnp.concatenate([_page_rows(vc[g], h), _page_rows(vc[g], DA_HEADS + h)],
                           axis=1).astype(BF16) for g in range(npg)]
          for h in range(DA_HEADS)]
    _, bias = alibi(npg * PAGE_SIZE, past - p * (npg * PAGE_SIZE))
    _attend(qb, kb, vb, bias, None, m_sc, l_sc, a_sc)

    @pl.when(p == pl.num_programs(1) - 1)
    def _():
        lam = _lambda(lam_ref[...], lam_init)
        for h in range(DA_HEADS):
            r1 = slice((2 * h) * dec, (2 * h + 1) * dec)
            r2 = slice((2 * h + 1) * dec, (2 * h + 2) * dec)
            o = _diff_finish(a_sc[r1, :], l_sc[r1, :], a_sc[r2, :], l_sc[r2, :], lam, g_ref[h], lam_init)
            o_ref[:, h * DA_V_DIM:(h + 1) * DA_V_DIM] = o


def _stored_pages_diff(pool):
    n_phys = pool.shape[0]
    x = pool.reshape(n_phys, PAGE_SIZE, DA_HEADS, 2, LANES)
    return jnp.transpose(x, (0, 1, 3, 2, 4)).reshape(n_phys, PAGE_SIZE * ROWS_PER_KEY, LANES)


def _sample_diff(dq, dk_new, dv_new, pool_k, pool_v, page_table, slopes, lam_vec, g_sub,
                 lam_init, dec, npg=PAGES_PER_STEP):
    n_seq, n_pages = page_table.shape
    npg = math.gcd(npg, n_pages)
    nb = 2 * DA_HEADS
    body = functools.partial(_sdiff_body, dec=dec, past=n_pages * PAGE_SIZE, lam_init=lam_init,
                             npg=npg)
    tok = lambda s, p, pt: (s, 0)
    const2 = lambda s, p, pt: (0, 0)

    def page(g):
        return pl.BlockSpec((1, PAGE_SIZE * ROWS_PER_KEY, LANES),
                            lambda s, p, pt: (pt[s * n_pages + p * npg + g], 0, 0))

    slope_rows = jnp.repeat(slopes, 2 * dec).reshape(nb * dec, 1)
    grid_spec = pltpu.PrefetchScalarGridSpec(
        num_scalar_prefetch=1,
        grid=(n_seq, n_pages // npg),
        in_specs=[
            pl.BlockSpec((nb * dec, 1), const2),
            pl.BlockSpec((4, DA_QK_DIM), const2),
            pl.BlockSpec((dec, DA_WIDTH), tok),
            pl.BlockSpec((dec, DA_WIDTH), tok),
            pl.BlockSpec((dec, DA_WIDTH), tok),
            pl.BlockSpec((DA_HEADS, 1, DA_V_DIM), lambda s, p, pt: (0, 0, 0)),
        ] + [page(g) for g in range(npg)] * 2,
        out_specs=pl.BlockSpec((dec, DA_WIDTH), tok),
        scratch_shapes=[
            pltpu.VMEM((nb * dec, 1), F32),
            pltpu.VMEM((nb * dec, 1), F32),
            pltpu.VMEM((nb * dec, DA_V_DIM), F32),
        ],
    )
    pages_k = _stored_pages_diff(pool_k)
    pages_v = _stored_pages_diff(pool_v)
    return pl.pallas_call(
        body,
        grid_spec=grid_spec,
        out_shape=jax.ShapeDtypeStruct((n_seq * dec, DA_WIDTH), F32),
        compiler_params=_cparams(("parallel", "arbitrary")),
        name="sample_diff",
    )(page_table.reshape(-1), slope_rows, lam_vec, dq, dk_new, dv_new,
      g_sub.reshape(DA_HEADS, 1, DA_V_DIM), *([pages_k] * npg), *([pages_v] * npg))


def _bf16_pieces(x):
    hi = x.astype(BF16).astype(F32)
    r1 = x - hi
    mid = r1.astype(BF16).astype(F32)
    lo = (r1 - mid).astype(BF16).astype(F32)
    return hi, mid, lo


def _sfox_body(pt_ref, q_ref, kn_ref, vn_ref, lfn_ref, *rest, dec, npg):
    kc, vc, lfc = rest[:npg], rest[npg:2 * npg], rest[2 * npg:3 * npg]
    o_ref = rest[3 * npg]
    m_sc, l_sc, a_sc, cq_sc, carry_sc = rest[3 * npg + 1:]
    p = pl.program_id(1)
    scale = FX_DIM ** -0.5
    qb = [(q_ref[:, h * FX_DIM:(h + 1) * FX_DIM] * scale).astype(BF16) for h in range(FX_HEADS)]

    @pl.when(p == 0)
    def _():
        m_sc[...] = jnp.full(m_sc.shape, NEG_BIG, F32)
        l_sc[...] = jnp.zeros(l_sc.shape, F32)
        a_sc[...] = jnp.zeros(a_sc.shape, F32)
        carry_sc[...] = jnp.zeros(carry_sc.shape, F32)
        lfn = lfn_ref[...]
        rows = [lfn[0:1]]
        for t in range(1, dec):
            rows.append(rows[-1] + lfn[t:t + 1])
        r8 = lax.broadcasted_iota(jnp.int32, (dec, FX_HEADS), 0)
        cum = jnp.zeros((dec, FX_HEADS), F32)
        for t in range(dec):
            cum = jnp.where(r8 == t, rows[t], cum)
        cq_sc[...] = cum
        kb = [[_pad_keys(kn_ref[:, h * FX_DIM:(h + 1) * FX_DIM]).astype(BF16)] for h in range(FX_HEADS)]
        vb = [[_pad_keys(vn_ref[:, h * FX_DIM:(h + 1) * FX_DIM]).astype(BF16)] for h in range(FX_HEADS)]
        bias = []
        for h in range(FX_HEADS):
            cqh = cum[:, h:h + 1]
            ck = jnp.concatenate([_col_to_row(cqh), jnp.zeros((1, PAGE_SIZE - dec), F32)], axis=1)
            bias.append(cqh - ck)
        row = lax.broadcasted_iota(jnp.int32, (FX_HEADS * dec, PAGE_SIZE), 0)
        col = lax.broadcasted_iota(jnp.int32, (FX_HEADS * dec, PAGE_SIZE), 1)
        _attend(qb, kb, vb, jnp.concatenate(bias, axis=0), col <= lax.rem(row, dec), m_sc, l_sc, a_sc)

    ki = lax.broadcasted_iota(jnp.int32, (PAGE_SIZE, PAGE_SIZE), 0)
    kj = lax.broadcasted_iota(jnp.int32, (PAGE_SIZE, PAGE_SIZE), 1)
    later = jnp.where(ki > kj, 1.0, 0.0).astype(BF16)
    lf = [lfc[g][0] for g in range(npg)]
    pieces = [_bf16_pieces(x) for x in lf]
    lhs = jnp.concatenate([pc[i] for i in range(3) for pc in pieces], axis=0).astype(BF16)
    prod = jnp.dot(lhs, later, preferred_element_type=F32)
    nh = FX_HEADS
    carry = carry_sc[...]
    after = []
    for g in range(npg):
        inside = (prod[g * nh:(g + 1) * nh] + prod[(npg + g) * nh:(npg + g + 1) * nh]
                  + prod[(2 * npg + g) * nh:(2 * npg + g + 1) * nh])
        after.append(inside + carry)
        carry = carry + inside[:, 0:1] + lf[g][:, 0:1]
    carry_sc[...] = carry
    cq = cq_sc[...]
    bias = jnp.concatenate(
        [cq[:, h:h + 1] + jnp.concatenate([a[h:h + 1, :] for a in after], axis=1)
         for h in range(FX_HEADS)], axis=0)
    kb = [[_page_rows(kc[g], h).astype(BF16) for g in range(npg)] for h in range(FX_HEADS)]
    vb = [[_page_rows(vc[g], h).astype(BF16) for g in range(npg)] for h in range(FX_HEADS)]
    _attend(qb, kb, vb, bias, None, m_sc, l_sc, a_sc)

    @pl.when(p == pl.num_programs(1) - 1)
    def _():
        o = a_sc[...] / l_sc[...]
        for h in range(FX_HEADS):
            o_ref[:, h * FX_DIM:(h + 1) * FX_DIM] = o[h * dec:(h + 1) * dec, :]


def _sample_fox(fq, fk_new, fv_new, lf_new, pool_k, pool_v, pool_lf_t, page_table, dec,
                npg=PAGES_PER_STEP):
    n_seq, n_pages = page_table.shape
    npg = math.gcd(npg, n_pages)
    n_phys = pool_k.shape[0]
    pages_k = pool_k.reshape(n_phys, PAGE_SIZE * ROWS_PER_KEY, LANES)
    pages_v = pool_v.reshape(n_phys, PAGE_SIZE * ROWS_PER_KEY, LANES)
    tok = lambda s, p, pt: (s, 0)

    def page(g, shape):
        return pl.BlockSpec(shape, lambda s, p, pt: (pt[s * n_pages + n_pages - 1 - (p * npg + g)], 0, 0))

    grid_spec = pltpu.PrefetchScalarGridSpec(
        num_scalar_prefetch=1,
        grid=(n_seq, n_pages // npg),
        in_specs=[
            pl.BlockSpec((dec, FX_WIDTH), tok),
            pl.BlockSpec((dec, FX_WIDTH), tok),
            pl.BlockSpec((dec, FX_WIDTH), tok),
            pl.BlockSpec((dec, FX_HEADS), tok),
        ] + [page(g, (1, PAGE_SIZE * ROWS_PER_KEY, LANES)) for g in range(npg)] * 2
          + [page(g, (1, FX_HEADS, PAGE_SIZE)) for g in range(npg)],
        out_specs=pl.BlockSpec((dec, FX_WIDTH), tok),
        scratch_shapes=[
            pltpu.VMEM((FX_HEADS * dec, 1), F32),
            pltpu.VMEM((FX_HEADS * dec, 1), F32),
            pltpu.VMEM((FX_HEADS * dec, FX_DIM), F32),
            pltpu.VMEM((dec, FX_HEADS), F32),
            pltpu.VMEM((FX_HEADS, 1), F32),
        ],
    )
    return pl.pallas_call(
        functools.partial(_sfox_body, dec=dec, npg=npg),
        grid_spec=grid_spec,
        out_shape=jax.ShapeDtypeStruct((n_seq * dec, FX_WIDTH), F32),
        compiler_params=_cparams(("parallel", "arbitrary")),
        name="sample_fox",
    )(page_table.reshape(-1), fq, fk_new, fv_new, lf_new,
      *([pages_k] * npg), *([pages_v] * npg), *([pool_lf_t] * npg))


def _xattn_body(q_ref, k_ref, v_ref, o_ref):
    scale = MEM_DIM ** -0.5
    s = _dot_nt(q_ref[...].astype(BF16), k_ref[0].astype(BF16)) * scale
    s = s - jnp.max(s, axis=-1, keepdims=True)
    e = jnp.exp(s)
    p = e / jnp.sum(e, axis=-1, keepdims=True)
    o_ref[...] = jnp.dot(p.astype(BF16), v_ref[0].astype(BF16), preferred_element_type=F32)


def _cross_attn(q, mem_k, mem_v, tq):
    groups, mem_len, _ = mem_k.shape
    m = q.shape[0]
    nq = m // (groups * tq)
    return pl.pallas_call(
        _xattn_body,
        grid=(groups, nq, MEM_HEADS),
        in_specs=[
            pl.BlockSpec((tq, MEM_DIM), lambda g, i, h: (g * nq + i, h)),
            pl.BlockSpec((1, mem_len, MEM_DIM), lambda g, i, h: (g, 0, h)),
            pl.BlockSpec((1, mem_len, MEM_DIM), lambda g, i, h: (g, 0, h)),
        ],
        out_specs=pl.BlockSpec((tq, MEM_DIM), lambda g, i, h: (g * nq + i, h)),
        out_shape=jax.ShapeDtypeStruct((m, D_MODEL), F32),
        compiler_params=_cparams(("parallel", "parallel", "arbitrary")),
    )(q, mem_k, mem_v)


MEM_ROWS_PER_TOKEN = MEM_HEADS * MEM_DIM // LANES


def _xattn_stored_body(q_ref, k_ref, v_ref, o_ref, *, mem_len):
    scale = MEM_DIM ** -0.5
    groups = MEM_DIM // LANES

    def head_block(ref, h):
        return jnp.concatenate(
            [ref[pl.ds(0, 1), pl.ds(c * MEM_HEADS + h, mem_len, stride=MEM_ROWS_PER_TOKEN), :][0]
             for c in range(groups)], axis=1).astype(BF16)

    for h in range(MEM_HEADS):
        q = (q_ref[:, h * MEM_DIM:(h + 1) * MEM_DIM] * scale).astype(BF16)
        s = _dot_nt(q, head_block(k_ref, h))
        s = s - jnp.max(s, axis=-1, keepdims=True)
        e = jnp.exp(s)
        p = e / jnp.sum(e, axis=-1, keepdims=True)
        o_ref[:, h * MEM_DIM:(h + 1) * MEM_DIM] = jnp.dot(
            p.astype(BF16), head_block(v_ref, h), preferred_element_type=F32)


def _stored_mem(cache):
    n_seq, mem_len = cache.shape[:2]
    x = cache.reshape(n_seq, mem_len, MEM_HEADS, MEM_DIM // LANES, LANES)
    return jnp.transpose(x, (0, 1, 3, 2, 4)).reshape(n_seq, mem_len * MEM_ROWS_PER_TOKEN, LANES)


def _cross_attn_stored(q, cache_k, cache_v, dec):
    n_seq, mem_len = cache_k.shape[:2]
    rows = mem_len * MEM_ROWS_PER_TOKEN
    return pl.pallas_call(
        functools.partial(_xattn_stored_body, mem_len=mem_len),
        grid=(n_seq,),
        in_specs=[
            pl.BlockSpec((dec, D_MODEL), lambda s: (s, 0)),
            pl.BlockSpec((1, rows, LANES), lambda s: (s, 0, 0)),
            pl.BlockSpec((1, rows, LANES), lambda s: (s, 0, 0)),
        ],
        out_specs=pl.BlockSpec((dec, D_MODEL), lambda s: (s, 0)),
        out_shape=jax.ShapeDtypeStruct((n_seq * dec, D_MODEL), F32),
        compiler_params=_cparams(("parallel",)),
        name="sample_cross_attn",
    )(q, _stored_mem(cache_k), _stored_mem(cache_v))


def _cand_layout():
    pos, valid = [], []
    for b in range(16):
        pos.append(b); valid.append(True)
    for a in range(1, 8):
        for b in range(8):
            pos.append(a * 16 + b); valid.append((a + 1) * (b + 1) <= PEER_TOPK)
    for a in range(8, 16):
        pos.append(a * 16); valid.append(True)
    return np.asarray(pos, np.float32)[:, None], np.asarray(valid, np.float32)[:, None]


def _top16(s, iota_f):
    rank = jnp.full(s.shape, float(PEER_TOPK), F32)
    vals = []
    work = s
    for a in range(PEER_TOPK):
        m = jnp.max(work, axis=0, keepdims=True)
        idx = jnp.min(jnp.where(work == m, iota_f, 1e9), axis=0, keepdims=True)
        hit = iota_f == idx
        rank = jnp.where(hit, float(a), rank)
        work = jnp.where(hit, -jnp.inf, work)
        vals.append(m)
    return rank, vals


def _rows_to_array(rows, n):
    tt = rows[0].shape[1]
    r = lax.broadcasted_iota(jnp.int32, (n, tt), 0)
    out = jnp.zeros((n, tt), F32)
    for a in range(n):
        out = jnp.where(r == a, rows[a], out)
    return out


def _psel_body(q_ref, k1_ref, k2_ref, pos_ref, valid_ref, w1_ref, n1_ref, w2_ref, r2_ref):
    tt = q_ref.shape[0]
    iota_f = lax.broadcasted_iota(jnp.int32, (N_KEYS, tt), 0).astype(F32)
    posb = jnp.broadcast_to(pos_ref[...], (pos_ref.shape[0], tt))
    validb = jnp.broadcast_to(valid_ref[...], posb.shape) > 0.5
    i16 = lax.broadcasted_iota(jnp.int32, (PEER_TOPK, tt), 0).astype(F32)
    for h in range(PEER_HEADS):
        lo = h * 2 * PEER_HALF
        qa = q_ref[:, lo:lo + PEER_HALF].astype(BF16)
        qb = q_ref[:, lo + PEER_HALF:lo + 2 * PEER_HALF].astype(BF16)
        s1 = _dot_nt(k1_ref[...], qa)
        s2 = _dot_nt(k2_ref[...], qb)
        rank1, v1 = _top16(s1, iota_f)
        rank2, v2 = _top16(s2, iota_f)
        v1a = _rows_to_array(v1, PEER_TOPK)
        v2a = _rows_to_array(v2, PEER_TOPK)
        blocks = [v1[0] + v2a]
        for a in range(1, 8):
            blocks.append(v1[a] + v2a[0:8])
        blocks.append(v1a[8:16] + v2[0])
        work = jnp.where(validb, jnp.concatenate(blocks, axis=0), -jnp.inf)
        top = v1[0] + v2[0]
        z = jnp.zeros((1, tt), F32)
        cnt = jnp.zeros((PEER_TOPK, tt), F32)
        for _ in range(PEER_TOPK):
            m = jnp.max(work, axis=0, keepdims=True)
            pmin = jnp.min(jnp.where(work == m, posb, 1e9), axis=0, keepdims=True)
            work = jnp.where(posb == pmin, -jnp.inf, work)
            z = z + jnp.exp(m - top)
            cnt = cnt + jnp.where(i16 == jnp.floor(pmin * (1.0 / 16.0)), 1.0, 0.0)
        n1 = jnp.zeros((N_KEYS, tt), F32)
        for a in range(PEER_TOPK):
            n1 = jnp.where(rank1 == float(a), cnt[a:a + 1], n1)
        w1_ref[h] = jnp.where(rank1 < PEER_TOPK, jnp.exp(s1 - v1[0]), 0.0)
        n1_ref[h] = n1
        w2_ref[h] = jnp.where(rank2 < PEER_TOPK, jnp.exp(s2 - v2[0]) / z, 0.0).astype(BF16)
        r2_ref[h] = rank2.astype(BF16)


def _peer_select(q, sub_k1, sub_k2, tt=256):
    m = q.shape[0]
    tt = min(tt, m)
    pos, valid = _cand_layout()
    n_c = pos.shape[0]
    tab = jax.ShapeDtypeStruct((PEER_HEADS, N_KEYS, m), F32)
    tab16 = jax.ShapeDtypeStruct((PEER_HEADS, N_KEYS, m), BF16)
    tab_spec = pl.BlockSpec((PEER_HEADS, N_KEYS, tt), lambda i: (0, 0, i))
    return pl.pallas_call(
        _psel_body,
        grid=(m // tt,),
        in_specs=[
            pl.BlockSpec((tt, q.shape[1]), lambda i: (i, 0)),
            pl.BlockSpec((N_KEYS, PEER_HALF), lambda i: (0, 0)),
            pl.BlockSpec((N_KEYS, PEER_HALF), lambda i: (0, 0)),
            pl.BlockSpec((n_c, 1), lambda i: (0, 0)),
            pl.BlockSpec((n_c, 1), lambda i: (0, 0)),
        ],
        out_specs=[tab_spec] * 4,
        out_shape=[tab, tab, tab16, tab16],
        compiler_params=_cparams(("parallel",)),
        name="peer_select",
    )(q, sub_k1.astype(BF16), sub_k2.astype(BF16), jnp.asarray(pos), jnp.asarray(valid))


def _gelu_tanh(x):
    return 0.5 * x * (1.0 + jnp.tanh(math.sqrt(2.0 / math.pi) * (x + 0.044715 * (x * x * x))))


def _pexp_body(x_ref, xn_ref, gl_ref, u_ref, vt_ref, w1_ref, n1_ref, w2_ref, r2_ref,
               y_ref, act_a, act_b, wt_a, wt_b, acc_sc, *, rows_per_step, tchunk, n_blocks):
    s = pl.program_id(1)
    tt = x_ref.shape[0]

    @pl.when(s == 0)
    def _():
        acc_sc[...] = jnp.zeros(acc_sc.shape, F32)
        for ref in (act_a, act_b, wt_a, wt_b):
            ref[...] = jnp.zeros(ref.shape, ref.dtype)

    gated = jnp.clip(s - 1, 0, n_blocks - 1)

    def stages(act_w, act_r, wt_w, wt_r):
        for rl in range(rows_per_step):
            r = gated * rows_per_step + rl
            for c in range(tt // tchunk):
                cs = slice(c * tchunk, (c + 1) * tchunk)
                gate = jnp.zeros((N_KEYS // 16, 16, tchunk), BF16)
                for h in range(PEER_HEADS):
                    w1 = jnp.broadcast_to(w1_ref[h, pl.ds(r, 1), cs], (16, tchunk)).astype(BF16)
                    n1 = jnp.broadcast_to(n1_ref[h, pl.ds(r, 1), cs], (16, tchunk)).astype(BF16)
                    r2 = r2_ref[h, :, cs].reshape(N_KEYS // 16, 16, tchunk)
                    w2 = w2_ref[h, :, cs].reshape(N_KEYS // 16, 16, tchunk)
                    gate = gate + w1[None] * jnp.where(r2 < n1[None], w2, jnp.zeros_like(w2))
                a = act_r[rl * N_KEYS:(rl + 1) * N_KEYS, cs]
                g32 = gate.reshape(N_KEYS, tchunk).astype(F32)
                wt_w[rl * N_KEYS:(rl + 1) * N_KEYS, cs] = (g32 * _gelu_tanh(a)).astype(BF16)
        acc_sc[...] += jnp.dot(vt_ref[...], wt_r[...], preferred_element_type=F32)
        act_w[...] = _dot_nt(u_ref[...], xn_ref[...])

    @pl.when(s % 2 == 0)
    def _():
        stages(act_a, act_b, wt_b, wt_a)

    @pl.when(s % 2 == 1)
    def _():
        stages(act_b, act_a, wt_a, wt_b)

    @pl.when(s == pl.num_programs(1) - 1)
    def _():
        x3 = x_ref[...] + acc_sc[...].T
        y_ref[...] = _rms(x3, gl_ref[...], NORM_EPS)


PEER_PIPELINE_DEPTH = 3


def _peer_experts(x, xn, g_final, u_bf, vt_bf, tabs, tt=512, rows_per_step=4):
    m, d = x.shape
    tt = min(tt, m)
    n_exp = u_bf.shape[0]
    ne = rows_per_step * N_KEYS
    n_blocks = n_exp // ne
    last = n_blocks - 1
    tab_spec = pl.BlockSpec((PEER_HEADS, N_KEYS, tt), lambda i, s: (0, 0, i))
    body = functools.partial(_pexp_body, rows_per_step=rows_per_step, tchunk=min(256, tt),
                             n_blocks=n_blocks)
    return pl.pallas_call(
        body,
        grid=(m // tt, n_blocks + PEER_PIPELINE_DEPTH - 1),
        in_specs=[
            pl.BlockSpec((tt, d), lambda i, s: (i, 0)),
            pl.BlockSpec((tt, d), lambda i, s: (i, 0)),
            pl.BlockSpec((1, d), lambda i, s: (0, 0)),
            pl.BlockSpec((ne, d), lambda i, s: (jnp.minimum(s, last), 0)),
            pl.BlockSpec((d, ne), lambda i, s: (0, jnp.clip(s - 2, 0, last))),
            tab_spec, tab_spec, tab_spec, tab_spec,
        ],
        out_specs=pl.BlockSpec((tt, d), lambda i, s: (i, 0)),
        out_shape=jax.ShapeDtypeStruct((m, d), F32),
        scratch_shapes=[
            pltpu.VMEM((ne, tt), F32),
            pltpu.VMEM((ne, tt), F32),
            pltpu.VMEM((ne, tt), BF16),
            pltpu.VMEM((ne, tt), BF16),
            pltpu.VMEM((d, tt), F32),
        ],
        compiler_params=_cparams(("parallel", "arbitrary")),
        name="peer_experts",
    )(x, xn, g_final.reshape(1, d), u_bf, vt_bf, *tabs)


def _alibi_slopes(n):
    return jnp.asarray(2.0 ** (-8.0 * np.arange(1, n + 1) / n), dtype=F32)


def _tail(x, od, of, cross_attn, w, u_bf, vt_bf):
    x1, xn1 = _mm([od, of], [w["out_d"], w["out_f"]], res=x, norm_gain=w["norm_mem"])
    q = _mm([xn1], [w["mem_q"]])
    o = cross_attn(q)
    x2, xn2 = _mm([o], [w["mem_o"]], res=x1, norm_gain=w["norm_ffn"])
    pq = _mm([xn2], [w["peer_q"]])
    tabs = _peer_select(pq, w["sub_k1"], w["sub_k2"])
    return _peer_experts(x2, xn2, w["norm_final"], u_bf, vt_bf, tabs)


def kernel(x_prompt, x_sample, mem_prompt, cache_diff_k, cache_diff_v, cache_fox_k, cache_fox_v,
           cache_fox_logf, cache_mem_k, cache_mem_v, page_table, norm_attn, w_in, b_forget,
           lambda_q1, lambda_k1, lambda_q2, lambda_k2, norm_sub, w_out, norm_mem, w_mem_q,
           w_mem_k, w_mem_v, w_mem_o, norm_ffn, w_peer_q, peer_sub_k1, peer_sub_k2, peer_u,
           peer_v, norm_final):
    depth = w_in.shape[0]
    assert depth == 1, "the residual chain below is written for a single layer"
    l = 0
    batch, seq, d = x_prompt.shape
    n_seq, dec, _ = x_sample.shape
    mem_len = mem_prompt.shape[1]
    lam_init = 0.8 - 0.6 * math.exp(-0.3 * l)
    slopes = _alibi_slopes(DA_HEADS)
    lam_vec = jnp.stack([lambda_q1[l], lambda_k1[l], lambda_q2[l], lambda_k2[l]])

    n_main = 6 * 1024
    w_in_bf = w_in[l, :, :n_main].astype(BF16)
    w_fl_bf = jnp.pad(w_in[l, :, n_main:], ((0, 0), (0, LANES - FX_HEADS))).astype(BF16)
    w = dict(
        out_d=w_out[l, :DA_WIDTH].astype(BF16), out_f=w_out[l, DA_WIDTH:].astype(BF16),
        mem_q=w_mem_q[l].astype(BF16), mem_o=w_mem_o[l].astype(BF16),
        peer_q=w_peer_q[l].astype(BF16), norm_mem=norm_mem[l], norm_ffn=norm_ffn[l],
        norm_final=norm_final, sub_k1=peer_sub_k1[l], sub_k2=peer_sub_k2[l])
    u_bf = peer_u[l].astype(BF16)
    vt_bf = peer_v[l].T.astype(BF16)

    xp = x_prompt.reshape(batch * seq, d)
    xs = x_sample.reshape(n_seq * dec, d)

    bias_f = jnp.zeros((LANES,), F32).at[:FX_HEADS].set(b_forget[l])
    (pdq, pdk, pdv, pfq, pfk, pfv), plf = _project(xp, norm_attn[l], w_in_bf, w_fl_bf, bias_f, 6)
    plf = plf[:, :FX_HEADS]
    cum = _cumsum_lanes(jnp.transpose(plf.reshape(batch, seq, FX_HEADS), (0, 2, 1)))
    od = _prompt_diff(pdq, pdk, pdv, slopes, lam_vec, norm_sub[l], batch, seq, lam_init)
    of = _prompt_fox(pfq, pfk, pfv, cum, batch, seq)
    mem2 = mem_prompt.reshape(batch * mem_len, d)
    pmk = _mm([mem2], [w_mem_k[l].astype(BF16)])
    pmv = _mm([mem2], [w_mem_v[l].astype(BF16)])
    prompt_xattn = lambda q: _cross_attn(q, pmk.reshape(batch, mem_len, d),
                                         pmv.reshape(batch, mem_len, d), 512)
    y_prompt = _tail(xp, od, of, prompt_xattn, w, u_bf, vt_bf)

    (sdq, sdk, sdv, sfq, sfk, sfv), slf = _project(xs, norm_attn[l], w_in_bf, w_fl_bf, bias_f, 6)
    slf = slf[:, :FX_HEADS]
    sod = _sample_diff(sdq, sdk, sdv, cache_diff_k[l], cache_diff_v[l], page_table, slopes,
                       lam_vec, norm_sub[l], lam_init, dec)
    pool_lf_t = jnp.transpose(cache_fox_logf[l], (0, 2, 1))
    sof = _sample_fox(sfq, sfk, sfv, slf, cache_fox_k[l], cache_fox_v[l], pool_lf_t,
                      page_table, dec)
    sample_xattn = lambda q: _cross_attn_stored(q, cache_mem_k[l], cache_mem_v[l], dec)
    y_sample = _tail(xs, sod, sof, sample_xattn, w, u_bf, vt_bf)

    return (y_prompt.reshape(batch, seq, d), y_sample.reshape(n_seq, dec, d),
            pdk.reshape(1, batch, seq, DA_HEADS, DA_V_DIM),
            pdv.reshape(1, batch, seq, DA_HEADS, DA_V_DIM),
            pfk.reshape(1, batch, seq, FX_HEADS, FX_DIM),
            pfv.reshape(1, batch, seq, FX_HEADS, FX_DIM),
            plf.reshape(1, batch, seq, FX_HEADS),
            pmk.reshape(1, batch, mem_len, MEM_HEADS, MEM_DIM),
            pmv.reshape(1, batch, mem_len, MEM_HEADS, MEM_DIM),
            sdk.reshape(1, n_seq, dec, DA_HEADS, DA_V_DIM),
            sdv.reshape(1, n_seq, dec, DA_HEADS, DA_V_DIM),
            sfk.reshape(1, n_seq, dec, FX_HEADS, FX_DIM),
            sfv.reshape(1, n_seq, dec, FX_HEADS, FX_DIM),
            slf.reshape(1, n_seq, dec, FX_HEADS))
```

```python
import functools
import math

import numpy as np
import jax
import jax.numpy as jnp
from jax import lax
from jax.experimental import pallas as pl
from jax.experimental.pallas import tpu as pltpu

F32 = jnp.float32
BF16 = jnp.bfloat16

D_MODEL = 2048
PAGE_SIZE = 128
DA_QK_DIM = 128
DA_V_DIM = 256
DA_HEADS = 4
DA_WIDTH = DA_HEADS * DA_V_DIM
FX_DIM = 128
FX_HEADS = 8
FX_WIDTH = FX_HEADS * FX_DIM
MEM_HEADS = 4
MEM_DIM = D_MODEL // MEM_HEADS
PEER_HEADS = 8
PEER_TOPK = 16
N_KEYS = 128
PEER_HALF = 128
NORM_EPS = 1e-6
SUBLN_EPS = 1e-5
NEG_BIG = -1e30

LANES = 128
VMEM_LIMIT = 56 * 1024 * 1024


def _cparams(sem):
    return pltpu.CompilerParams(dimension_semantics=sem, vmem_limit_bytes=VMEM_LIMIT)


def _dot_nt(a, b):
    return lax.dot_general(a, b, (((1,), (1,)), ((), ())), preferred_element_type=F32)


def _rms(x, g, eps):
    return x * lax.rsqrt(jnp.mean(x * x, axis=-1, keepdims=True) + eps) * g


def _resident(shape):
    return pl.BlockSpec(shape, lambda i: (0,) * len(shape), pipeline_mode=pl.Buffered(1))


def _log_sigmoid(z):
    return jnp.minimum(z, 0.0) - jnp.log1p(jnp.exp(-jnp.abs(z)))


def _proj_body(x_ref, g_ref, w_ref, wl_ref, b_ref, *o_refs, eps):
    xb = _rms(x_ref[...], g_ref[...], eps).astype(BF16)
    width = o_refs[0].shape[1]
    for k, o_ref in enumerate(o_refs[:-1]):
        o_ref[...] = jnp.dot(xb, w_ref[:, k * width:(k + 1) * width], preferred_element_type=F32)
    z = jnp.dot(xb, wl_ref[...], preferred_element_type=F32) + b_ref[...]
    o_refs[-1][...] = _log_sigmoid(z)


def _project(x, g, w_bf, wl_bf, bias, n_groups, tm=256):
    m, d = x.shape
    tm = min(tm, m)
    width = w_bf.shape[1] // n_groups
    nl = wl_bf.shape[1]
    row = lambda i: (i, 0)
    outs = pl.pallas_call(
        functools.partial(_proj_body, eps=NORM_EPS),
        grid=(m // tm,),
        in_specs=[pl.BlockSpec((tm, d), row), _resident((1, d)), _resident(w_bf.shape),
                  _resident(wl_bf.shape), _resident((1, nl))],
        out_specs=[pl.BlockSpec((tm, width), row)] * n_groups + [pl.BlockSpec((tm, nl), row)],
        out_shape=[jax.ShapeDtypeStruct((m, width), F32)] * n_groups
                  + [jax.ShapeDtypeStruct((m, nl), F32)],
        compiler_params=_cparams(("parallel",)),
        name="in_proj",
    )(x, g.reshape(1, d), w_bf, wl_bf, bias.reshape(1, nl))
    return outs[:-1], outs[-1]


def _mm_body(*refs, n_in, has_res, emit_norm, eps):
    xs = refs[:n_in]
    ws = refs[n_in:2 * n_in]
    pos = 2 * n_in
    r_ref = g_ref = None
    if has_res:
        r_ref = refs[pos]; pos += 1
    if emit_norm:
        g_ref = refs[pos]; pos += 1
    o_ref = refs[pos]
    acc = None
    for x_ref, w_ref in zip(xs, ws):
        d = jnp.dot(x_ref[...].astype(BF16), w_ref[...], preferred_element_type=F32)
        acc = d if acc is None else acc + d
    if has_res:
        acc = acc + r_ref[...]
    o_ref[...] = acc
    if emit_norm:
        refs[pos + 1][...] = _rms(acc, g_ref[...], eps).astype(BF16)


def _mm(xs, ws, *, res=None, norm_gain=None, eps=NORM_EPS, tm=512):
    m = xs[0].shape[0]
    n = ws[0].shape[1]
    tm = min(tm, m)
    assert m % tm == 0
    row = lambda i: (i, 0)
    in_specs = [pl.BlockSpec((tm, x.shape[1]), row) for x in xs]
    in_specs += [_resident(w.shape) for w in ws]
    args = list(xs) + list(ws)
    if res is not None:
        in_specs.append(pl.BlockSpec((tm, n), row))
        args.append(res)
    out_specs = [pl.BlockSpec((tm, n), row)]
    out_shape = [jax.ShapeDtypeStruct((m, n), F32)]
    if norm_gain is not None:
        in_specs.append(_resident((1, n)))
        args.append(norm_gain.reshape(1, n))
        out_specs.append(pl.BlockSpec((tm, n), row))
        out_shape.append(jax.ShapeDtypeStruct((m, n), BF16))
    body = functools.partial(_mm_body, n_in=len(xs), has_res=res is not None,
                             emit_norm=norm_gain is not None, eps=eps)
    outs = pl.pallas_call(
        body, grid=(m // tm,), in_specs=in_specs, out_specs=out_specs, out_shape=out_shape,
        compiler_params=_cparams(("parallel",)), name="token_matmul",
    )(*args)
    return outs if norm_gain is not None else outs[0]


def _lambda(lv, lam_init):
    s1 = jnp.sum(lv[0:1] * lv[1:2], axis=-1, keepdims=True)
    s2 = jnp.sum(lv[2:3] * lv[3:4], axis=-1, keepdims=True)
    return jnp.exp(s1) - jnp.exp(s2) + lam_init


def _online(s, m_old, l_old):
    m_new = jnp.maximum(m_old, jnp.max(s, axis=-1, keepdims=True))
    p = jnp.exp(s - m_new)
    alpha = jnp.exp(m_old - m_new)
    l_new = alpha * l_old + jnp.sum(p, axis=-1, keepdims=True)
    return p, alpha, m_new, l_new


def _col_to_row(col):
    n = col.shape[0]
    eye = lax.broadcasted_iota(jnp.int32, (n, n), 0) == lax.broadcasted_iota(jnp.int32, (n, n), 1)
    return jnp.sum(jnp.where(eye, col, 0.0), axis=0, keepdims=True)


def _row_to_col(row):
    n = row.shape[1]
    eye = lax.broadcasted_iota(jnp.int32, (n, n), 0) == lax.broadcasted_iota(jnp.int32, (n, n), 1)
    return jnp.sum(jnp.where(eye, row, 0.0), axis=1, keepdims=True)


def _diff_finish(a1, l1, a2, l2, lam, g, lam_init):
    o = a1 / l1 - lam * (a2 / l2)
    return _rms(o, g, SUBLN_EPS) * (1.0 - lam_init)


PDIFF_HEADS_PER_STEP = 2
PFOX_HEADS_PER_STEP = 4


def _pdiff_body(slope_ref, lam_ref, q_ref, k_ref, v_ref, g_ref, o_ref, *, tq, lam_init, hb):
    hg = pl.program_id(1)
    qi = pl.program_id(2)
    lam = _lambda(lam_ref[...], lam_init)
    scale = DA_QK_DIM ** -0.5
    nmap = 2 * hb
    qs = [(q_ref[:, i * DA_QK_DIM:(i + 1) * DA_QK_DIM] * scale).astype(BF16) for i in range(nmap)]
    slopes = [slope_ref[hg * hb + hh] for hh in range(hb)]
    rc = (lax.broadcasted_iota(jnp.int32, (tq, tq), 0)
          - lax.broadcasted_iota(jnp.int32, (tq, tq), 1))

    def block(j, carry, diagonal):
        start = pl.multiple_of(j * tq, tq)
        dist = rc + (qi - j) * tq
        distf = dist.astype(F32)
        out = []
        for hh in range(hb):
            bias = -slopes[hh] * distf
            v = v_ref[pl.ds(start, tq), hh * DA_V_DIM:(hh + 1) * DA_V_DIM].astype(BF16)
            for mp in range(2):
                i = 2 * hh + mp
                m, l, a = carry[i]
                k = k_ref[pl.ds(start, tq), i * DA_QK_DIM:(i + 1) * DA_QK_DIM].astype(BF16)
                s = _dot_nt(qs[i], k) + bias
                if diagonal:
                    s = jnp.where(dist >= 0, s, -jnp.inf)
                p, al, m, l = _online(s, m, l)
                a = al * a + jnp.dot(p.astype(BF16), v, preferred_element_type=F32)
                out.append((m, l, a))
        return tuple(out)

    init = (jnp.full((tq, 1), NEG_BIG, F32), jnp.zeros((tq, 1), F32), jnp.zeros((tq, DA_V_DIM), F32))
    res = lax.fori_loop(0, qi, lambda j, c: block(j, c, False), (init,) * nmap)
    res = block(qi, res, True)
    for hh in range(hb):
        (_, l1, a1), (_, l2, a2) = res[2 * hh], res[2 * hh + 1]
        o_ref[:, hh * DA_V_DIM:(hh + 1) * DA_V_DIM] = _diff_finish(a1, l1, a2, l2, lam, g_ref[hh], lam_init)


def _prompt_diff(dq, dk, dv, slopes, lam_vec, g_sub, batch, seq, lam_init, tq=256,
                 hb=PDIFF_HEADS_PER_STEP):
    nq = seq // tq
    w = hb * DA_V_DIM
    body = functools.partial(_pdiff_body, tq=tq, lam_init=lam_init, hb=hb)
    return pl.pallas_call(
        body,
        grid=(batch, DA_HEADS // hb, nq),
        in_specs=[
            pl.BlockSpec(memory_space=pltpu.SMEM),
            pl.BlockSpec((4, DA_QK_DIM), lambda b, h, i: (0, 0)),
            pl.BlockSpec((tq, w), lambda b, h, i: (b * nq + i, h)),
            pl.BlockSpec((seq, w), lambda b, h, i: (b, h)),
            pl.BlockSpec((seq, w), lambda b, h, i: (b, h)),
            pl.BlockSpec((hb, 1, DA_V_DIM), lambda b, h, i: (h, 0, 0)),
        ],
        out_specs=pl.BlockSpec((tq, w), lambda b, h, i: (b * nq + i, h)),
        out_shape=jax.ShapeDtypeStruct((batch * seq, DA_WIDTH), F32),
        compiler_params=_cparams(("parallel", "parallel", "arbitrary")),
        name="prompt_diff",
    )(slopes, lam_vec, dq, dk, dv, g_sub.reshape(DA_HEADS, 1, DA_V_DIM))


def _cumsum_body(x_ref, o_ref):
    x = x_ref[0]
    n = x.shape[1]
    lane = lax.broadcasted_iota(jnp.int32, x.shape, 1)
    s = 1
    while s < n:
        x = x + jnp.where(lane >= s, pltpu.roll(x, s, axis=1), 0.0)
        s *= 2
    o_ref[0] = x


def _cumsum_lanes(x):
    b, h, n = x.shape
    return pl.pallas_call(
        _cumsum_body,
        grid=(b,),
        in_specs=[pl.BlockSpec((1, h, n), lambda i: (i, 0, 0))],
        out_specs=pl.BlockSpec((1, h, n), lambda i: (i, 0, 0)),
        out_shape=jax.ShapeDtypeStruct((b, h, n), F32),
        compiler_params=_cparams(("parallel",)),
        name="logf_cumsum",
    )(x)


def _pfox_body(q_ref, k_ref, v_ref, c_ref, o_ref, *, tq, hb):
    qi = pl.program_id(2)
    scale = FX_DIM ** -0.5
    qs = [(q_ref[:, hh * FX_DIM:(hh + 1) * FX_DIM] * scale).astype(BF16) for hh in range(hb)]
    rc = (lax.broadcasted_iota(jnp.int32, (tq, tq), 0)
          - lax.broadcasted_iota(jnp.int32, (tq, tq), 1))
    qstart = pl.multiple_of(qi * tq, tq)
    cqs = [_row_to_col(c_ref[0, hh, :, pl.ds(qstart, tq)]) for hh in range(hb)]

    def block(j, carry, diagonal):
        start = pl.multiple_of(j * tq, tq)
        out = []
        for hh in range(hb):
            m, l, a = carry[hh]
            k = k_ref[pl.ds(start, tq), hh * FX_DIM:(hh + 1) * FX_DIM].astype(BF16)
            v = v_ref[pl.ds(start, tq), hh * FX_DIM:(hh + 1) * FX_DIM].astype(BF16)
            ck = c_ref[0, hh, :, pl.ds(start, tq)]
            s = _dot_nt(qs[hh], k) + cqs[hh] - ck
            if diagonal:
                s = jnp.where(rc >= 0, s, -jnp.inf)
            p, al, m, l = _online(s, m, l)
            a = al * a + jnp.dot(p.astype(BF16), v, preferred_element_type=F32)
            out.append((m, l, a))
        return tuple(out)

    init = (jnp.full((tq, 1), NEG_BIG, F32), jnp.zeros((tq, 1), F32), jnp.zeros((tq, FX_DIM), F32))
    res = lax.fori_loop(0, qi, lambda j, c: block(j, c, False), (init,) * hb)
    res = block(qi, res, True)
    for hh in range(hb):
        _, l, a = res[hh]
        o_ref[:, hh * FX_DIM:(hh + 1) * FX_DIM] = a / l


def _prompt_fox(fq, fk, fv, cum, batch, seq, tq=256, hb=PFOX_HEADS_PER_STEP):
    nq = seq // tq
    w = hb * FX_DIM
    return pl.pallas_call(
        functools.partial(_pfox_body, tq=tq, hb=hb),
        grid=(batch, FX_HEADS // hb, nq),
        in_specs=[
            pl.BlockSpec((tq, w), lambda b, h, i: (b * nq + i, h)),
            pl.BlockSpec((seq, w), lambda b, h, i: (b, h)),
            pl.BlockSpec((seq, w), lambda b, h, i: (b, h)),
            pl.BlockSpec((1, hb, 1, seq), lambda b, h, i: (b, h, 0, 0)),
        ],
        out_specs=pl.BlockSpec((tq, w), lambda b, h, i: (b * nq + i, h)),
        out_shape=jax.ShapeDtypeStruct((batch * seq, FX_WIDTH), F32),
        compiler_params=_cparams(("parallel", "parallel", "arbitrary")),
        name="prompt_fox",
    )(fq, fk, fv, cum.reshape(batch, FX_HEADS, 1, seq))


PAGES_PER_STEP = 16
ROWS_PER_KEY = 8


def _page_rows(ref, sub):
    return ref[pl.ds(0, 1), pl.ds(sub, PAGE_SIZE, stride=ROWS_PER_KEY), :][0]


def _pad_keys(x):
    return jnp.concatenate([x, jnp.zeros((PAGE_SIZE - x.shape[0], x.shape[1]), x.dtype)], axis=0)


def _attend(q_blocks, k_blocks, v_blocks, bias, mask, m_sc, l_sc, a_sc):
    dec = q_blocks[0].shape[0]
    s = jnp.concatenate(
        [jnp.concatenate([_dot_nt(q, k) for k in ks], axis=1) for q, ks in zip(q_blocks, k_blocks)],
        axis=0) + bias
    if mask is not None:
        s = jnp.where(mask, s, -jnp.inf)
    m_old = m_sc[...]
    m_new = jnp.maximum(m_old, jnp.max(s, axis=1, keepdims=True))
    p = jnp.exp(s - m_new)
    alpha = jnp.exp(m_old - m_new)
    m_sc[...] = m_new
    l_sc[...] = alpha * l_sc[...] + jnp.sum(p, axis=1, keepdims=True)
    r = dec * (len(q_blocks) // len(v_blocks))
    outs = []
    for j, vs in enumerate(v_blocks):
        o = None
        for g, v in enumerate(vs):
            pj = p[j * r:(j + 1) * r, g * PAGE_SIZE:(g + 1) * PAGE_SIZE].astype(BF16)
            d = jnp.dot(pj, v, preferred_element_type=F32)
            o = d if o is None else o + d
        outs.append(o)
    a_sc[...] = alpha * a_sc[...] + jnp.concatenate(outs, axis=0)


def _sdiff_body(pt_ref, slope_ref, lam_ref, q_ref, kn_ref, vn_ref, g_ref, *rest,
                dec, past, lam_init, npg):
    kc, vc = rest[:npg], rest[npg:2 * npg]
    o_ref = rest[2 * npg]
    m_sc, l_sc, a_sc = rest[2 * npg + 1:]
    p = pl.program_id(1)
    nb = 2 * DA_HEADS
    scale = DA_QK_DIM ** -0.5
    qb = [(q_ref[:, i * DA_QK_DIM:(i + 1) * DA_QK_DIM] * scale).astype(BF16) for i in range(nb)]
    slope = slope_ref[...]

    def alibi(width, base):
        row = lax.broadcasted_iota(jnp.int32, (nb * dec, width), 0)
        col = lax.broadcasted_iota(jnp.int32, (nb * dec, width), 1)
        dist = base + lax.rem(row, dec) - col
        return dist, -slope * dist.astype(F32)

    @pl.when(p == 0)
    def _():
        m_sc[...] = jnp.full(m_sc.shape, NEG_BIG, F32)
        l_sc[...] = jnp.zeros(l_sc.shape, F32)
        a_sc[...] = jnp.zeros(a_sc.shape, F32)
        kb = [[_pad_keys(kn_ref[:, i * DA_QK_DIM:(i + 1) * DA_QK_DIM]).astype(BF16)] for i in range(nb)]
        vb = [[_pad_keys(vn_ref[:, h * DA_V_DIM:(h + 1) * DA_V_DIM]).astype(BF16)]
              for h in range(DA_HEADS)]
        dist, bias = alibi(PAGE_SIZE, 0)
        _attend(qb, kb, vb, bias, dist >= 0, m_sc, l_sc, a_sc)

    kb = [[_page_rows(kc[g], mp * DA_HEADS + h).astype(BF16) for g in range(npg)]
          for h in range(DA_HEADS) for mp in range(2)]
    vb = [[jnp.concatenate([_page_rows(vc[g], h), _page_rows(vc[g], DA_HEADS + h)],
                           axis=1).astype(BF16) for g in range(npg)]
          for h in range(DA_HEADS)]
    _, bias = alibi(npg * PAGE_SIZE, past - p * (npg * PAGE_SIZE))
    _attend(qb, kb, vb, bias, None, m_sc, l_sc, a_sc)

    @pl.when(p == pl.num_programs(1) - 1)
    def _():
        lam = _lambda(lam_ref[...], lam_init)
        for h in range(DA_HEADS):
            r1 = slice((2 * h) * dec, (2 * h + 1) * dec)
            r2 = slice((2 * h + 1) * dec, (2 * h + 2) * dec)
            o = _diff_finish(a_sc[r1, :], l_sc[r1, :], a_sc[r2, :], l_sc[r2, :], lam, g_ref[h], lam_init)
            o_ref[:, h * DA_V_DIM:(h + 1) * DA_V_DIM] = o


def _stored_pages_diff(pool):
    n_phys = pool.shape[0]
    x = pool.reshape(n_phys, PAGE_SIZE, DA_HEADS, 2, LANES)
    return jnp.transpose(x, (0, 1, 3, 2, 4)).reshape(n_phys, PAGE_SIZE * ROWS_PER_KEY, LANES)


def _sample_diff(dq, dk_new, dv_new, pool_k, pool_v, page_table, slopes, lam_vec, g_sub,
                 lam_init, dec, npg=PAGES_PER_STEP):
    n_seq, n_pages = page_table.shape
    npg = math.gcd(npg, n_pages)
    nb = 2 * DA_HEADS
    body = functools.partial(_sdiff_body, dec=dec, past=n_pages * PAGE_SIZE, lam_init=lam_init,
                             npg=npg)
    tok = lambda s, p, pt: (s, 0)
    const2 = lambda s, p, pt: (0, 0)

    def page(g):
        return pl.BlockSpec((1, PAGE_SIZE * ROWS_PER_KEY, LANES),
                            lambda s, p, pt: (pt[s * n_pages + p * npg + g], 0, 0))

    slope_rows = jnp.repeat(slopes, 2 * dec).reshape(nb * dec, 1)
    grid_spec = pltpu.PrefetchScalarGridSpec(
        num_scalar_prefetch=1,
        grid=(n_seq, n_pages // npg),
        in_specs=[
            pl.BlockSpec((nb * dec, 1), const2),
            pl.BlockSpec((4, DA_QK_DIM), const2),
            pl.BlockSpec((dec, DA_WIDTH), tok),
            pl.BlockSpec((dec, DA_WIDTH), tok),
            pl.BlockSpec((dec, DA_WIDTH), tok),
            pl.BlockSpec((DA_HEADS, 1, DA_V_DIM), lambda s, p, pt: (0, 0, 0)),
        ] + [page(g) for g in range(npg)] * 2,
        out_specs=pl.BlockSpec((dec, DA_WIDTH), tok),
        scratch_shapes=[
            pltpu.VMEM((nb * dec, 1), F32),
            pltpu.VMEM((nb * dec, 1), F32),
            pltpu.VMEM((nb * dec, DA_V_DIM), F32),
        ],
    )
    pages_k = _stored_pages_diff(pool_k)
    pages_v = _stored_pages_diff(pool_v)
    return pl.pallas_call(
        body,
        grid_spec=grid_spec,
        out_shape=jax.ShapeDtypeStruct((n_seq * dec, DA_WIDTH), F32),
        compiler_params=_cparams(("parallel", "arbitrary")),
        name="sample_diff",
    )(page_table.reshape(-1), slope_rows, lam_vec, dq, dk_new, dv_new,
      g_sub.reshape(DA_HEADS, 1, DA_V_DIM), *([pages_k] * npg), *([pages_v] * npg))


def _bf16_pieces(x):
    hi = x.astype(BF16).astype(F32)
    r1 = x - hi
    mid = r1.astype(BF16).astype(F32)
    lo = (r1 - mid).astype(BF16).astype(F32)
    return hi, mid, lo


def _sfox_body(pt_ref, q_ref, kn_ref, vn_ref, lfn_ref, *rest, dec, npg):
    kc, vc, lfc = rest[:npg], rest[npg:2 * npg], rest[2 * npg:3 * npg]
    o_ref = rest[3 * npg]
    m_sc, l_sc, a_sc, cq_sc, carry_sc = rest[3 * npg + 1:]
    p = pl.program_id(1)
    scale = FX_DIM ** -0.5
    qb = [(q_ref[:, h * FX_DIM:(h + 1) * FX_DIM] * scale).astype(BF16) for h in range(FX_HEADS)]

    @pl.when(p == 0)
    def _():
        m_sc[...] = jnp.full(m_sc.shape, NEG_BIG, F32)
        l_sc[...] = jnp.zeros(l_sc.shape, F32)
        a_sc[...] = jnp.zeros(a_sc.shape, F32)
        carry_sc[...] = jnp.zeros(carry_sc.shape, F32)
        lfn = lfn_ref[...]
        rows = [lfn[0:1]]
        for t in range(1, dec):
            rows.append(rows[-1] + lfn[t:t + 1])
        r8 = lax.broadcasted_iota(jnp.int32, (dec, FX_HEADS), 0)
        cum = jnp.zeros((dec, FX_HEADS), F32)
        for t in range(dec):
            cum = jnp.where(r8 == t, rows[t], cum)
        cq_sc[...] = cum
        kb = [[_pad_keys(kn_ref[:, h * FX_DIM:(h + 1) * FX_DIM]).astype(BF16)] for h in range(FX_HEADS)]
        vb = [[_pad_keys(vn_ref[:, h * FX_DIM:(h + 1) * FX_DIM]).astype(BF16)] for h in range(FX_HEADS)]
        bias = []
        for h in range(FX_HEADS):
            cqh = cum[:, h:h + 1]
            ck = jnp.concatenate([_col_to_row(cqh), jnp.zeros((1, PAGE_SIZE - dec), F32)], axis=1)
            bias.append(cqh - ck)
        row = lax.broadcasted_iota(jnp.int32, (FX_HEADS * dec, PAGE_SIZE), 0)
        col = lax.broadcasted_iota(jnp.int32, (FX_HEADS * dec, PAGE_SIZE), 1)
        _attend(qb, kb, vb, jnp.concatenate(bias, axis=0), col <= lax.rem(row, dec), m_sc, l_sc, a_sc)

    ki = lax.broadcasted_iota(jnp.int32, (PAGE_SIZE, PAGE_SIZE), 0)
    kj = lax.broadcasted_iota(jnp.int32, (PAGE_SIZE, PAGE_SIZE), 1)
    later = jnp.where(ki > kj, 1.0, 0.0).astype(BF16)
    lf = [lfc[g][0] for g in range(npg)]
    pieces = [_bf16_pieces(x) for x in lf]
    lhs = jnp.concatenate([pc[i] for i in range(3) for pc in pieces], axis=0).astype(BF16)
    prod = jnp.dot(lhs, later, preferred_element_type=F32)
    nh = FX_HEADS
    carry = carry_sc[...]
    after = []
    for g in range(npg):
        inside = (prod[g * nh:(g + 1) * nh] + prod[(npg + g) * nh:(npg + g + 1) * nh]
                  + prod[(2 * npg + g) * nh:(2 * npg + g + 1) * nh])
        after.append(inside + carry)
        carry = carry + inside[:, 0:1] + lf[g][:, 0:1]
    carry_sc[...] = carry
    cq = cq_sc[...]
    bias = jnp.concatenate(
        [cq[:, h:h + 1] + jnp.concatenate([a[h:h + 1, :] for a in after], axis=1)
         for h in range(FX_HEADS)], axis=0)
    kb = [[_page_rows(kc[g], h).astype(BF16) for g in range(npg)] for h in range(FX_HEADS)]
    vb = [[_page_rows(vc[g], h).astype(BF16) for g in range(npg)] for h in range(FX_HEADS)]
    _attend(qb, kb, vb, bias, None, m_sc, l_sc, a_sc)

    @pl.when(p == pl.num_programs(1) - 1)
    def _():
        o = a_sc[...] / l_sc[...]
        for h in range(FX_HEADS):
            o_ref[:, h * FX_DIM:(h + 1) * FX_DIM] = o[h * dec:(h + 1) * dec, :]


def _sample_fox(fq, fk_new, fv_new, lf_new, pool_k, pool_v, pool_lf_t, page_table, dec,
                npg=PAGES_PER_STEP):
    n_seq, n_pages = page_table.shape
    npg = math.gcd(npg, n_pages)
    n_phys = pool_k.shape[0]
    pages_k = pool_k.reshape(n_phys, PAGE_SIZE * ROWS_PER_KEY, LANES)
    pages_v = pool_v.reshape(n_phys, PAGE_SIZE * ROWS_PER_KEY, LANES)
    tok = lambda s, p, pt: (s, 0)

    def page(g, shape):
        return pl.BlockSpec(shape, lambda s, p, pt: (pt[s * n_pages + n_pages - 1 - (p * npg + g)], 0, 0))

    grid_spec = pltpu.PrefetchScalarGridSpec(
        num_scalar_prefetch=1,
        grid=(n_seq, n_pages // npg),
        in_specs=[
            pl.BlockSpec((dec, FX_WIDTH), tok),
            pl.BlockSpec((dec, FX_WIDTH), tok),
            pl.BlockSpec((dec, FX_WIDTH), tok),
            pl.BlockSpec((dec, FX_HEADS), tok),
        ] + [page(g, (1, PAGE_SIZE * ROWS_PER_KEY, LANES)) for g in range(npg)] * 2
          + [page(g, (1, FX_HEADS, PAGE_SIZE)) for g in range(npg)],
        out_specs=pl.BlockSpec((dec, FX_WIDTH), tok),
        scratch_shapes=[
            pltpu.VMEM((FX_HEADS * dec, 1), F32),
            pltpu.VMEM((FX_HEADS * dec, 1), F32),
            pltpu.VMEM((FX_HEADS * dec, FX_DIM), F32),
            pltpu.VMEM((dec, FX_HEADS), F32),
            pltpu.VMEM((FX_HEADS, 1), F32),
        ],
    )
    return pl.pallas_call(
        functools.partial(_sfox_body, dec=dec, npg=npg),
        grid_spec=grid_spec,
        out_shape=jax.ShapeDtypeStruct((n_seq * dec, FX_WIDTH), F32),
        compiler_params=_cparams(("parallel", "arbitrary")),
        name="sample_fox",
    )(page_table.reshape(-1), fq, fk_new, fv_new, lf_new,
      *([pages_k] * npg), *([pages_v] * npg), *([pool_lf_t] * npg))


def _xattn_body(q_ref, k_ref, v_ref, o_ref):
    scale = MEM_DIM ** -0.5
    s = _dot_nt(q_ref[...].astype(BF16), k_ref[0].astype(BF16)) * scale
    s = s - jnp.max(s, axis=-1, keepdims=True)
    e = jnp.exp(s)
    p = e / jnp.sum(e, axis=-1, keepdims=True)
    o_ref[...] = jnp.dot(p.astype(BF16), v_ref[0].astype(BF16), preferred_element_type=F32)


def _cross_attn(q, mem_k, mem_v, tq):
    groups, mem_len, _ = mem_k.shape
    m = q.shape[0]
    nq = m // (groups * tq)
    return pl.pallas_call(
        _xattn_body,
        grid=(groups, nq, MEM_HEADS),
        in_specs=[
            pl.BlockSpec((tq, MEM_DIM), lambda g, i, h: (g * nq + i, h)),
            pl.BlockSpec((1, mem_len, MEM_DIM), lambda g, i, h: (g, 0, h)),
            pl.BlockSpec((1, mem_len, MEM_DIM), lambda g, i, h: (g, 0, h)),
        ],
        out_specs=pl.BlockSpec((tq, MEM_DIM), lambda g, i, h: (g * nq + i, h)),
        out_shape=jax.ShapeDtypeStruct((m, D_MODEL), F32),
        compiler_params=_cparams(("parallel", "parallel", "arbitrary")),
    )(q, mem_k, mem_v)


MEM_ROWS_PER_TOKEN = MEM_HEADS * MEM_DIM // LANES


def _xattn_stored_body(q_ref, k_ref, v_ref, o_ref, *, mem_len):
    scale = MEM_DIM ** -0.5
    groups = MEM_DIM // LANES

    def head_block(ref, h):
        return jnp.concatenate(
            [ref[pl.ds(0, 1), pl.ds(c * MEM_HEADS + h, mem_len, stride=MEM_ROWS_PER_TOKEN), :][0]
             for c in range(groups)], axis=1).astype(BF16)

    for h in range(MEM_HEADS):
        q = (q_ref[:, h * MEM_DIM:(h + 1) * MEM_DIM] * scale).astype(BF16)
        s = _dot_nt(q, head_block(k_ref, h))
        s = s - jnp.max(s, axis=-1, keepdims=True)
        e = jnp.exp(s)
        p = e / jnp.sum(e, axis=-1, keepdims=True)
        o_ref[:, h * MEM_DIM:(h + 1) * MEM_DIM] = jnp.dot(
            p.astype(BF16), head_block(v_ref, h), preferred_element_type=F32)


def _stored_mem(cache):
    n_seq, mem_len = cache.shape[:2]
    x = cache.reshape(n_seq, mem_len, MEM_HEADS, MEM_DIM // LANES, LANES)
    return jnp.transpose(x, (0, 1, 3, 2, 4)).reshape(n_seq, mem_len * MEM_ROWS_PER_TOKEN, LANES)


def _cross_attn_stored(q, cache_k, cache_v, dec):
    n_seq, mem_len = cache_k.shape[:2]
    rows = mem_len * MEM_ROWS_PER_TOKEN
    return pl.pallas_call(
        functools.partial(_xattn_stored_body, mem_len=mem_len),
        grid=(n_seq,),
        in_specs=[
            pl.BlockSpec((dec, D_MODEL), lambda s: (s, 0)),
            pl.BlockSpec((1, rows, LANES), lambda s: (s, 0, 0)),
            pl.BlockSpec((1, rows, LANES), lambda s: (s, 0, 0)),
        ],
        out_specs=pl.BlockSpec((dec, D_MODEL), lambda s: (s, 0)),
        out_shape=jax.ShapeDtypeStruct((n_seq * dec, D_MODEL), F32),
        compiler_params=_cparams(("parallel",)),
        name="sample_cross_attn",
    )(q, _stored_mem(cache_k), _stored_mem(cache_v))


def _cand_layout():
    pos, valid = [], []
    for b in range(16):
        pos.append(b); valid.append(True)
    for a in range(1, 8):
        for b in range(8):
            pos.append(a * 16 + b); valid.append((a + 1) * (b + 1) <= PEER_TOPK)
    for a in range(8, 16):
        pos.append(a * 16); valid.append(True)
    return np.asarray(pos, np.float32)[:, None], np.asarray(valid, np.float32)[:, None]


def _top16(s, iota_f):
    rank = jnp.full(s.shape, float(PEER_TOPK), F32)
    vals = []
    work = s
    for a in range(PEER_TOPK):
        m = jnp.max(work, axis=0, keepdims=True)
        idx = jnp.min(jnp.where(work == m, iota_f, 1e9), axis=0, keepdims=True)
        hit = iota_f == idx
        rank = jnp.where(hit, float(a), rank)
        work = jnp.where(hit, -jnp.inf, work)
        vals.append(m)
    return rank, vals


def _rows_to_array(rows, n):
    tt = rows[0].shape[1]
    r = lax.broadcasted_iota(jnp.int32, (n, tt), 0)
    out = jnp.zeros((n, tt), F32)
    for a in range(n):
        out = jnp.where(r == a, rows[a], out)
    return out


def _psel_body(q_ref, k1_ref, k2_ref, pos_ref, valid_ref, w1_ref, n1_ref, w2_ref, r2_ref):
    tt = q_ref.shape[0]
    iota_f = lax.broadcasted_iota(jnp.int32, (N_KEYS, tt), 0).astype(F32)
    posb = jnp.broadcast_to(pos_ref[...], (pos_ref.shape[0], tt))
    validb = jnp.broadcast_to(valid_ref[...], posb.shape) > 0.5
    i16 = lax.broadcasted_iota(jnp.int32, (PEER_TOPK, tt), 0).astype(F32)
    for h in range(PEER_HEADS):
        lo = h * 2 * PEER_HALF
        qa = q_ref[:, lo:lo + PEER_HALF].astype(BF16)
        qb = q_ref[:, lo + PEER_HALF:lo + 2 * PEER_HALF].astype(BF16)
        s1 = _dot_nt(k1_ref[...], qa)
        s2 = _dot_nt(k2_ref[...], qb)
        rank1, v1 = _top16(s1, iota_f)
        rank2, v2 = _top16(s2, iota_f)
        v1a = _rows_to_array(v1, PEER_TOPK)
        v2a = _rows_to_array(v2, PEER_TOPK)
        blocks = [v1[0] + v2a]
        for a in range(1, 8):
            blocks.append(v1[a] + v2a[0:8])
        blocks.append(v1a[8:16] + v2[0])
        work = jnp.where(validb, jnp.concatenate(blocks, axis=0), -jnp.inf)
        top = v1[0] + v2[0]
        z = jnp.zeros((1, tt), F32)
        cnt = jnp.zeros((PEER_TOPK, tt), F32)
        for _ in range(PEER_TOPK):
            m = jnp.max(work, axis=0, keepdims=True)
            pmin = jnp.min(jnp.where(work == m, posb, 1e9), axis=0, keepdims=True)
            work = jnp.where(posb == pmin, -jnp.inf, work)
            z = z + jnp.exp(m - top)
            cnt = cnt + jnp.where(i16 == jnp.floor(pmin * (1.0 / 16.0)), 1.0, 0.0)
        n1 = jnp.zeros((N_KEYS, tt), F32)
        for a in range(PEER_TOPK):
            n1 = jnp.where(rank1 == float(a), cnt[a:a + 1], n1)
        w1_ref[h] = jnp.where(rank1 < PEER_TOPK, jnp.exp(s1 - v1[0]), 0.0)
        n1_ref[h] = n1
        w2_ref[h] = jnp.where(rank2 < PEER_TOPK, jnp.exp(s2 - v2[0]) / z, 0.0).astype(BF16)
        r2_ref[h] = rank2.astype(BF16)


def _peer_select(q, sub_k1, sub_k2, tt=256):
    m = q.shape[0]
    tt = min(tt, m)
    pos, valid = _cand_layout()
    n_c = pos.shape[0]
    tab = jax.ShapeDtypeStruct((PEER_HEADS, N_KEYS, m), F32)
    tab16 = jax.ShapeDtypeStruct((PEER_HEADS, N_KEYS, m), BF16)
    tab_spec = pl.BlockSpec((PEER_HEADS, N_KEYS, tt), lambda i: (0, 0, i))
    return pl.pallas_call(
        _psel_body,
        grid=(m // tt,),
        in_specs=[
            pl.BlockSpec((tt, q.shape[1]), lambda i: (i, 0)),
            pl.BlockSpec((N_KEYS, PEER_HALF), lambda i: (0, 0)),
            pl.BlockSpec((N_KEYS, PEER_HALF), lambda i: (0, 0)),
            pl.BlockSpec((n_c, 1), lambda i: (0, 0)),
            pl.BlockSpec((n_c, 1), lambda i: (0, 0)),
        ],
        out_specs=[tab_spec] * 4,
        out_shape=[tab, tab, tab16, tab16],
        compiler_params=_cparams(("parallel",)),
        name="peer_select",
    )(q, sub_k1.astype(BF16), sub_k2.astype(BF16), jnp.asarray(pos), jnp.asarray(valid))


def _gelu_tanh(x):
    return 0.5 * x * (1.0 + jnp.tanh(math.sqrt(2.0 / math.pi) * (x + 0.044715 * (x * x * x))))


def _pexp_body(x_ref, xn_ref, gl_ref, u_ref, vt_ref, w1_ref, n1_ref, w2_ref, r2_ref,
               y_ref, act_a, act_b, wt_a, wt_b, acc_sc, *, rows_per_step, tchunk, n_blocks):
    s = pl.program_id(1)
    tt = x_ref.shape[0]

    @pl.when(s == 0)
    def _():
        acc_sc[...] = jnp.zeros(acc_sc.shape, F32)
        for ref in (act_a, act_b, wt_a, wt_b):
            ref[...] = jnp.zeros(ref.shape, ref.dtype)

    gated = jnp.clip(s - 1, 0, n_blocks - 1)

    def stages(act_w, act_r, wt_w, wt_r):
        for rl in range(rows_per_step):
            r = gated * rows_per_step + rl
            for c in range(tt // tchunk):
                cs = slice(c * tchunk, (c + 1) * tchunk)
                gate = jnp.zeros((N_KEYS // 16, 16, tchunk), BF16)
                for h in range(PEER_HEADS):
                    w1 = jnp.broadcast_to(w1_ref[h, pl.ds(r, 1), cs], (16, tchunk)).astype(BF16)
                    n1 = jnp.broadcast_to(n1_ref[h, pl.ds(r, 1), cs], (16, tchunk)).astype(BF16)
                    r2 = r2_ref[h, :, cs].reshape(N_KEYS // 16, 16, tchunk)
                    w2 = w2_ref[h, :, cs].reshape(N_KEYS // 16, 16, tchunk)
                    gate = gate + w1[None] * jnp.where(r2 < n1[None], w2, jnp.zeros_like(w2))
                a = act_r[rl * N_KEYS:(rl + 1) * N_KEYS, cs]
                g32 = gate.reshape(N_KEYS, tchunk).astype(F32)
                wt_w[rl * N_KEYS:(rl + 1) * N_KEYS, cs] = (g32 * _gelu_tanh(a)).astype(BF16)
        acc_sc[...] += jnp.dot(vt_ref[...], wt_r[...], preferred_element_type=F32)
        act_w[...] = _dot_nt(u_ref[...], xn_ref[...])

    @pl.when(s % 2 == 0)
    def _():
        stages(act_a, act_b, wt_b, wt_a)

    @pl.when(s % 2 == 1)
    def _():
        stages(act_b, act_a, wt_a, wt_b)

    @pl.when(s == pl.num_programs(1) - 1)
    def _():
        x3 = x_ref[...] + acc_sc[...].T
        y_ref[...] = _rms(x3, gl_ref[...], NORM_EPS)


PEER_PIPELINE_DEPTH = 3


def _peer_experts(x, xn, g_final, u_bf, vt_bf, tabs, tt=512, rows_per_step=4):
    m, d = x.shape
    tt = min(tt, m)
    n_exp = u_bf.shape[0]
    ne = rows_per_step * N_KEYS
    n_blocks = n_exp // ne
    last = n_blocks - 1
    tab_spec = pl.BlockSpec((PEER_HEADS, N_KEYS, tt), lambda i, s: (0, 0, i))
    body = functools.partial(_pexp_body, rows_per_step=rows_per_step, tchunk=min(256, tt),
                             n_blocks=n_blocks)
    return pl.pallas_call(
        body,
        grid=(m // tt, n_blocks + PEER_PIPELINE_DEPTH - 1),
        in_specs=[
            pl.BlockSpec((tt, d), lambda i, s: (i, 0)),
            pl.BlockSpec((tt, d), lambda i, s: (i, 0)),
            pl.BlockSpec((1, d), lambda i, s: (0, 0)),
            pl.BlockSpec((ne, d), lambda i, s: (jnp.minimum(s, last), 0)),
            pl.BlockSpec((d, ne), lambda i, s: (0, jnp.clip(s - 2, 0, last))),
            tab_spec, tab_spec, tab_spec, tab_spec,
        ],
        out_specs=pl.BlockSpec((tt, d), lambda i, s: (i, 0)),
        out_shape=jax.ShapeDtypeStruct((m, d), F32),
        scratch_shapes=[
            pltpu.VMEM((ne, tt), F32),
            pltpu.VMEM((ne, tt), F32),
            pltpu.VMEM((ne, tt), BF16),
            pltpu.VMEM((ne, tt), BF16),
            pltpu.VMEM((d, tt), F32),
        ],
        compiler_params=_cparams(("parallel", "arbitrary")),
        name="peer_experts",
    )(x, xn, g_final.reshape(1, d), u_bf, vt_bf, *tabs)


def _alibi_slopes(n):
    return jnp.asarray(2.0 ** (-8.0 * np.arange(1, n + 1) / n), dtype=F32)


def _tail(x, od, of, cross_attn, w, u_bf, vt_bf):
    x1, xn1 = _mm([od, of], [w["out_d"], w["out_f"]], res=x, norm_gain=w["norm_mem"])
    q = _mm([xn1], [w["mem_q"]])
    o = cross_attn(q)
    x2, xn2 = _mm([o], [w["mem_o"]], res=x1, norm_gain=w["norm_ffn"])
    pq = _mm([xn2], [w["peer_q"]])
    tabs = _peer_select(pq, w["sub_k1"], w["sub_k2"])
    return _peer_experts(x2, xn2, w["norm_final"], u_bf, vt_bf, tabs)


def kernel(x_prompt, x_sample, mem_prompt, cache_diff_k, cache_diff_v, cache_fox_k, cache_fox_v,
           cache_fox_logf, cache_mem_k, cache_mem_v, page_table, norm_attn, w_in, b_forget,
           lambda_q1, lambda_k1, lambda_q2, lambda_k2, norm_sub, w_out, norm_mem, w_mem_q,
           w_mem_k, w_mem_v, w_mem_o, norm_ffn, w_peer_q, peer_sub_k1, peer_sub_k2, peer_u,
           peer_v, norm_final):
    depth = w_in.shape[0]
    assert depth == 1, "the residual chain below is written for a single layer"
    l = 0
    batch, seq, d = x_prompt.shape
    n_seq, dec, _ = x_sample.shape
    mem_len = mem_prompt.shape[1]
    lam_init = 0.8 - 0.6 * math.exp(-0.3 * l)
    slopes = _alibi_slopes(DA_HEADS)
    lam_vec = jnp.stack([lambda_q1[l], lambda_k1[l], lambda_q2[l], lambda_k2[l]])

    n_main = 6 * 1024
    w_in_bf = w_in[l, :, :n_main].astype(BF16)
    w_fl_bf = jnp.pad(w_in[l, :, n_main:], ((0, 0), (0, LANES - FX_HEADS))).astype(BF16)
    w = dict(
        out_d=w_out[l, :DA_WIDTH].astype(BF16), out_f=w_out[l, DA_WIDTH:].astype(BF16),
        mem_q=w_mem_q[l].astype(BF16), mem_o=w_mem_o[l].astype(BF16),
        peer_q=w_peer_q[l].astype(BF16), norm_mem=norm_mem[l], norm_ffn=norm_ffn[l],
        norm_final=norm_final, sub_k1=peer_sub_k1[l], sub_k2=peer_sub_k2[l])
    u_bf = peer_u[l].astype(BF16)
    vt_bf = peer_v[l].T.astype(BF16)

    xp = x_prompt.reshape(batch * seq, d)
    xs = x_sample.reshape(n_seq * dec, d)

    bias_f = jnp.zeros((LANES,), F32).at[:FX_HEADS].set(b_forget[l])
    (pdq, pdk, pdv, pfq, pfk, pfv), plf = _project(xp, norm_attn[l], w_in_bf, w_fl_bf, bias_f, 6)
    plf = plf[:, :FX_HEADS]
    cum = _cumsum_lanes(jnp.transpose(plf.reshape(batch, seq, FX_HEADS), (0, 2, 1)))
    od = _prompt_diff(pdq, pdk, pdv, slopes, lam_vec, norm_sub[l], batch, seq, lam_init)
    of = _prompt_fox(pfq, pfk, pfv, cum, batch, seq)
    mem2 = mem_prompt.reshape(batch * mem_len, d)
    pmk = _mm([mem2], [w_mem_k[l].astype(BF16)])
    pmv = _mm([mem2], [w_mem_v[l].astype(BF16)])
    prompt_xattn = lambda q: _cross_attn(q, pmk.reshape(batch, mem_len, d),
                                         pmv.reshape(batch, mem_len, d), min(seq, 2048))
    y_prompt = _tail(xp, od, of, prompt_xattn, w, u_bf, vt_bf)

    (sdq, sdk, sdv, sfq, sfk, sfv), slf = _project(xs, norm_attn[l], w_in_bf, w_fl_bf, bias_f, 6)
    slf = slf[:, :FX_HEADS]
    sod = _sample_diff(sdq, sdk, sdv, cache_diff_k[l], cache_diff_v[l], page_table, slopes,
                       lam_vec, norm_sub[l], lam_init, dec)
    pool_lf_t = jnp.transpose(cache_fox_logf[l], (0, 2, 1))
    sof = _sample_fox(sfq, sfk, sfv, slf, cache_fox_k[l], cache_fox_v[l], pool_lf_t,
                      page_table, dec)
    sample_xattn = lambda q: _cross_attn_stored(q, cache_mem_k[l], cache_mem_v[l], dec)
    y_sample = _tail(xs, sod, sof, sample_xattn, w, u_bf, vt_bf)

    return (y_prompt.reshape(batch, seq, d), y_sample.reshape(n_seq, dec, d),
            pdk.reshape(1, batch, seq, DA_HEADS, DA_V_DIM),
            pdv.reshape(1, batch, seq, DA_HEADS, DA_V_DIM),
            pfk.reshape(1, batch, seq, FX_HEADS, FX_DIM),
            pfv.reshape(1, batch, seq, FX_HEADS, FX_DIM),
            plf.reshape(1, batch, seq, FX_HEADS),
            pmk.reshape(1, batch, mem_len, MEM_HEADS, MEM_DIM),
            pmv.reshape(1, batch, mem_len, MEM_HEADS, MEM_DIM),
            sdk.reshape(1, n_seq, dec, DA_HEADS, DA_V_DIM),
            sdv.reshape(1, n_seq, dec, DA_HEADS, DA_V_DIM),
            sfk.reshape(1, n_seq, dec, FX_HEADS, FX_DIM),
            sfv.reshape(1, n_seq, dec, FX_HEADS, FX_DIM),
            slf.reshape(1, n_seq, dec, FX_HEADS))
```

```python
import functools
import math

import numpy as np
import jax
import jax.numpy as jnp
from jax import lax
from jax.experimental import pallas as pl
from jax.experimental.pallas import tpu as pltpu

F32 = jnp.float32
BF16 = jnp.bfloat16

D_MODEL = 2048
PAGE_SIZE = 128
DA_QK_DIM = 128
DA_V_DIM = 256
DA_HEADS = 4
DA_WIDTH = DA_HEADS * DA_V_DIM
FX_DIM = 128
FX_HEADS = 8
FX_WIDTH = FX_HEADS * FX_DIM
MEM_HEADS = 4
MEM_DIM = D_MODEL // MEM_HEADS
PEER_HEADS = 8
PEER_TOPK = 16
N_KEYS = 128
PEER_HALF = 128
NORM_EPS = 1e-6
SUBLN_EPS = 1e-5
NEG_BIG = -1e30

LANES = 128
VMEM_LIMIT = 56 * 1024 * 1024


def _cparams(sem):
    return pltpu.CompilerParams(dimension_semantics=sem, vmem_limit_bytes=VMEM_LIMIT)


def _dot_nt(a, b):
    return lax.dot_general(a, b, (((1,), (1,)), ((), ())), preferred_element_type=F32)


def _rms(x, g, eps):
    return x * lax.rsqrt(jnp.mean(x * x, axis=-1, keepdims=True) + eps) * g


def _resident(shape):
    return pl.BlockSpec(shape, lambda i: (0,) * len(shape), pipeline_mode=pl.Buffered(1))


def _log_sigmoid(z):
    return jnp.minimum(z, 0.0) - jnp.log1p(jnp.exp(-jnp.abs(z)))


def _proj_body(x_ref, g_ref, w_ref, wl_ref, b_ref, *o_refs, eps):
    xb = _rms(x_ref[...], g_ref[...], eps).astype(BF16)
    width = o_refs[0].shape[1]
    for k, o_ref in enumerate(o_refs[:-1]):
        o_ref[...] = jnp.dot(xb, w_ref[:, k * width:(k + 1) * width], preferred_element_type=F32)
    z = jnp.dot(xb, wl_ref[...], preferred_element_type=F32) + b_ref[...]
    o_refs[-1][...] = _log_sigmoid(z)


def _project(x, g, w_bf, wl_bf, bias, n_groups, tm=256):
    m, d = x.shape
    tm = min(tm, m)
    width = w_bf.shape[1] // n_groups
    nl = wl_bf.shape[1]
    row = lambda i: (i, 0)
    outs = pl.pallas_call(
        functools.partial(_proj_body, eps=NORM_EPS),
        grid=(m // tm,),
        in_specs=[pl.BlockSpec((tm, d), row), _resident((1, d)), _resident(w_bf.shape),
                  _resident(wl_bf.shape), _resident((1, nl))],
        out_specs=[pl.BlockSpec((tm, width), row)] * n_groups + [pl.BlockSpec((tm, nl), row)],
        out_shape=[jax.ShapeDtypeStruct((m, width), F32)] * n_groups
                  + [jax.ShapeDtypeStruct((m, nl), F32)],
        compiler_params=_cparams(("parallel",)),
        name="in_proj",
    )(x, g.reshape(1, d), w_bf, wl_bf, bias.reshape(1, nl))
    return outs[:-1], outs[-1]


def _mm_body(*refs, n_in, has_res, emit_norm, eps):
    xs = refs[:n_in]
    ws = refs[n_in:2 * n_in]
    pos = 2 * n_in
    r_ref = g_ref = None
    if has_res:
        r_ref = refs[pos]; pos += 1
    if emit_norm:
        g_ref = refs[pos]; pos += 1
    o_ref = refs[pos]
    acc = None
    for x_ref, w_ref in zip(xs, ws):
        d = jnp.dot(x_ref[...].astype(BF16), w_ref[...], preferred_element_type=F32)
        acc = d if acc is None else acc + d
    if has_res:
        acc = acc + r_ref[...]
    o_ref[...] = acc
    if emit_norm:
        refs[pos + 1][...] = _rms(acc, g_ref[...], eps).astype(BF16)


def _mm(xs, ws, *, res=None, norm_gain=None, eps=NORM_EPS, tm=512):
    m = xs[0].shape[0]
    n = ws[0].shape[1]
    tm = min(tm, m)
    assert m % tm == 0
    row = lambda i: (i, 0)
    in_specs = [pl.BlockSpec((tm, x.shape[1]), row) for x in xs]
    in_specs += [_resident(w.shape) for w in ws]
    args = list(xs) + list(ws)
    if res is not None:
        in_specs.append(pl.BlockSpec((tm, n), row))
        args.append(res)
    out_specs = [pl.BlockSpec((tm, n), row)]
    out_shape = [jax.ShapeDtypeStruct((m, n), F32)]
    if norm_gain is not None:
        in_specs.append(_resident((1, n)))
        args.append(norm_gain.reshape(1, n))
        out_specs.append(pl.BlockSpec((tm, n), row))
        out_shape.append(jax.ShapeDtypeStruct((m, n), BF16))
    body = functools.partial(_mm_body, n_in=len(xs), has_res=res is not None,
                             emit_norm=norm_gain is not None, eps=eps)
    outs = pl.pallas_call(
        body, grid=(m // tm,), in_specs=in_specs, out_specs=out_specs, out_shape=out_shape,
        compiler_params=_cparams(("parallel",)), name="token_matmul",
    )(*args)
    return outs if norm_gain is not None else outs[0]


def _lambda(lv, lam_init):
    s1 = jnp.sum(lv[0:1] * lv[1:2], axis=-1, keepdims=True)
    s2 = jnp.sum(lv[2:3] * lv[3:4], axis=-1, keepdims=True)
    return jnp.exp(s1) - jnp.exp(s2) + lam_init


def _online(s, m_old, l_old):
    m_new = jnp.maximum(m_old, jnp.max(s, axis=-1, keepdims=True))
    p = jnp.exp(s - m_new)
    alpha = jnp.exp(m_old - m_new)
    l_new = alpha * l_old + jnp.sum(p, axis=-1, keepdims=True)
    return p, alpha, m_new, l_new


def _col_to_row(col):
    n = col.shape[0]
    eye = lax.broadcasted_iota(jnp.int32, (n, n), 0) == lax.broadcasted_iota(jnp.int32, (n, n), 1)
    return jnp.sum(jnp.where(eye, col, 0.0), axis=0, keepdims=True)


def _row_to_col(row):
    n = row.shape[1]
    eye = lax.broadcasted_iota(jnp.int32, (n, n), 0) == lax.broadcasted_iota(jnp.int32, (n, n), 1)
    return jnp.sum(jnp.where(eye, row, 0.0), axis=1, keepdims=True)


def _diff_finish(a1, l1, a2, l2, lam, g, lam_init):
    o = a1 / l1 - lam * (a2 / l2)
    return _rms(o, g, SUBLN_EPS) * (1.0 - lam_init)


PDIFF_HEADS_PER_STEP = 4
PFOX_HEADS_PER_STEP = 8


def _pdiff_body(slope_ref, lam_ref, q_ref, k_ref, v_ref, g_ref, o_ref, *, tq, lam_init, hb):
    hg = pl.program_id(1)
    qi = pl.program_id(2)
    lam = _lambda(lam_ref[...], lam_init)
    scale = DA_QK_DIM ** -0.5
    nmap = 2 * hb
    qs = [(q_ref[:, i * DA_QK_DIM:(i + 1) * DA_QK_DIM] * scale).astype(BF16) for i in range(nmap)]
    slopes = [slope_ref[hg * hb + hh] for hh in range(hb)]
    rc = (lax.broadcasted_iota(jnp.int32, (tq, tq), 0)
          - lax.broadcasted_iota(jnp.int32, (tq, tq), 1))

    def block(j, carry, diagonal):
        start = pl.multiple_of(j * tq, tq)
        dist = rc + (qi - j) * tq
        distf = dist.astype(F32)
        out = []
        for hh in range(hb):
            bias = -slopes[hh] * distf
            v = v_ref[pl.ds(start, tq), hh * DA_V_DIM:(hh + 1) * DA_V_DIM].astype(BF16)
            for mp in range(2):
                i = 2 * hh + mp
                m, l, a = carry[i]
                k = k_ref[pl.ds(start, tq), i * DA_QK_DIM:(i + 1) * DA_QK_DIM].astype(BF16)
                s = _dot_nt(qs[i], k) + bias
                if diagonal:
                    s = jnp.where(dist >= 0, s, -jnp.inf)
                p, al, m, l = _online(s, m, l)
                a = al * a + jnp.dot(p.astype(BF16), v, preferred_element_type=F32)
                out.append((m, l, a))
        return tuple(out)

    init = (jnp.full((tq, 1), NEG_BIG, F32), jnp.zeros((tq, 1), F32), jnp.zeros((tq, DA_V_DIM), F32))
    res = lax.fori_loop(0, qi, lambda j, c: block(j, c, False), (init,) * nmap)
    res = block(qi, res, True)
    for hh in range(hb):
        (_, l1, a1), (_, l2, a2) = res[2 * hh], res[2 * hh + 1]
        o_ref[:, hh * DA_V_DIM:(hh + 1) * DA_V_DIM] = _diff_finish(a1, l1, a2, l2, lam, g_ref[hh], lam_init)


def _prompt_diff(dq, dk, dv, slopes, lam_vec, g_sub, batch, seq, lam_init, tq=256,
                 hb=PDIFF_HEADS_PER_STEP):
    nq = seq // tq
    w = hb * DA_V_DIM
    body = functools.partial(_pdiff_body, tq=tq, lam_init=lam_init, hb=hb)
    return pl.pallas_call(
        body,
        grid=(batch, DA_HEADS // hb, nq),
        in_specs=[
            pl.BlockSpec(memory_space=pltpu.SMEM),
            pl.BlockSpec((4, DA_QK_DIM), lambda b, h, i: (0, 0)),
            pl.BlockSpec((tq, w), lambda b, h, i: (b * nq + i, h)),
            pl.BlockSpec((seq, w), lambda b, h, i: (b, h)),
            pl.BlockSpec((seq, w), lambda b, h, i: (b, h)),
            pl.BlockSpec((hb, 1, DA_V_DIM), lambda b, h, i: (h, 0, 0)),
        ],
        out_specs=pl.BlockSpec((tq, w), lambda b, h, i: (b * nq + i, h)),
        out_shape=jax.ShapeDtypeStruct((batch * seq, DA_WIDTH), F32),
        compiler_params=_cparams(("parallel", "parallel", "arbitrary")),
        name="prompt_diff",
    )(slopes, lam_vec, dq, dk, dv, g_sub.reshape(DA_HEADS, 1, DA_V_DIM))


def _cumsum_body(x_ref, o_ref):
    x = x_ref[0]
    n = x.shape[1]
    lane = lax.broadcasted_iota(jnp.int32, x.shape, 1)
    s = 1
    while s < n:
        x = x + jnp.where(lane >= s, pltpu.roll(x, s, axis=1), 0.0)
        s *= 2
    o_ref[0] = x


def _cumsum_lanes(x):
    b, h, n = x.shape
    return pl.pallas_call(
        _cumsum_body,
        grid=(b,),
        in_specs=[pl.BlockSpec((1, h, n), lambda i: (i, 0, 0))],
        out_specs=pl.BlockSpec((1, h, n), lambda i: (i, 0, 0)),
        out_shape=jax.ShapeDtypeStruct((b, h, n), F32),
        compiler_params=_cparams(("parallel",)),
        name="logf_cumsum",
    )(x)


def _pfox_body(q_ref, k_ref, v_ref, c_ref, o_ref, *, tq, hb):
    qi = pl.program_id(2)
    scale = FX_DIM ** -0.5
    qs = [(q_ref[:, hh * FX_DIM:(hh + 1) * FX_DIM] * scale).astype(BF16) for hh in range(hb)]
    rc = (lax.broadcasted_iota(jnp.int32, (tq, tq), 0)
          - lax.broadcasted_iota(jnp.int32, (tq, tq), 1))
    qstart = pl.multiple_of(qi * tq, tq)
    cqs = [_row_to_col(c_ref[0, hh, :, pl.ds(qstart, tq)]) for hh in range(hb)]

    def block(j, carry, diagonal):
        start = pl.multiple_of(j * tq, tq)
        out = []
        for hh in range(hb):
            m, l, a = carry[hh]
            k = k_ref[pl.ds(start, tq), hh * FX_DIM:(hh + 1) * FX_DIM].astype(BF16)
            v = v_ref[pl.ds(start, tq), hh * FX_DIM:(hh + 1) * FX_DIM].astype(BF16)
            ck = c_ref[0, hh, :, pl.ds(start, tq)]
            s = _dot_nt(qs[hh], k) + cqs[hh] - ck
            if diagonal:
                s = jnp.where(rc >= 0, s, -jnp.inf)
            p, al, m, l = _online(s, m, l)
            a = al * a + jnp.dot(p.astype(BF16), v, preferred_element_type=F32)
            out.append((m, l, a))
        return tuple(out)

    init = (jnp.full((tq, 1), NEG_BIG, F32), jnp.zeros((tq, 1), F32), jnp.zeros((tq, FX_DIM), F32))
    res = lax.fori_loop(0, qi, lambda j, c: block(j, c, False), (init,) * hb)
    res = block(qi, res, True)
    for hh in range(hb):
        _, l, a = res[hh]
        o_ref[:, hh * FX_DIM:(hh + 1) * FX_DIM] = a / l


def _prompt_fox(fq, fk, fv, cum, batch, seq, tq=256, hb=PFOX_HEADS_PER_STEP):
    nq = seq // tq
    w = hb * FX_DIM
    return pl.pallas_call(
        functools.partial(_pfox_body, tq=tq, hb=hb),
        grid=(batch, FX_HEADS // hb, nq),
        in_specs=[
            pl.BlockSpec((tq, w), lambda b, h, i: (b * nq + i, h)),
            pl.BlockSpec((seq, w), lambda b, h, i: (b, h)),
            pl.BlockSpec((seq, w), lambda b, h, i: (b, h)),
            pl.BlockSpec((1, hb, 1, seq), lambda b, h, i: (b, h, 0, 0)),
        ],
        out_specs=pl.BlockSpec((tq, w), lambda b, h, i: (b * nq + i, h)),
        out_shape=jax.ShapeDtypeStruct((batch * seq, FX_WIDTH), F32),
        compiler_params=_cparams(("parallel", "parallel", "arbitrary")),
        name="prompt_fox",
    )(fq, fk, fv, cum.reshape(batch, FX_HEADS, 1, seq))


PAGES_PER_STEP = 16
ROWS_PER_KEY = 8


def _page_rows(ref, sub):
    return ref[pl.ds(0, 1), pl.ds(sub, PAGE_SIZE, stride=ROWS_PER_KEY), :][0]


def _pad_keys(x):
    return jnp.concatenate([x, jnp.zeros((PAGE_SIZE - x.shape[0], x.shape[1]), x.dtype)], axis=0)


def _attend(q_blocks, k_blocks, v_blocks, bias, mask, m_sc, l_sc, a_sc):
    dec = q_blocks[0].shape[0]
    s = jnp.concatenate(
        [jnp.concatenate([_dot_nt(q, k) for k in ks], axis=1) for q, ks in zip(q_blocks, k_blocks)],
        axis=0) + bias
    if mask is not None:
        s = jnp.where(mask, s, -jnp.inf)
    m_old = m_sc[...]
    m_new = jnp.maximum(m_old, jnp.max(s, axis=1, keepdims=True))
    p = jnp.exp(s - m_new)
    alpha = jnp.exp(m_old - m_new)
    m_sc[...] = m_new
    l_sc[...] = alpha * l_sc[...] + jnp.sum(p, axis=1, keepdims=True)
    r = dec * (len(q_blocks) // len(v_blocks))
    outs = []
    for j, vs in enumerate(v_blocks):
        o = None
        for g, v in enumerate(vs):
            pj = p[j * r:(j + 1) * r, g * PAGE_SIZE:(g + 1) * PAGE_SIZE].astype(BF16)
            d = jnp.dot(pj, v, preferred_element_type=F32)
            o = d if o is None else o + d
        outs.append(o)
    a_sc[...] = alpha * a_sc[...] + jnp.concatenate(outs, axis=0)


def _sdiff_body(pt_ref, slope_ref, lam_ref, q_ref, kn_ref, vn_ref, g_ref, *rest,
                dec, past, lam_init, npg):
    kc, vc = rest[:npg], rest[npg:2 * npg]
    o_ref = rest[2 * npg]
    m_sc, l_sc, a_sc = rest[2 * npg + 1:]
    p = pl.program_id(1)
    nb = 2 * DA_HEADS
    scale = DA_QK_DIM ** -0.5
    qb = [(q_ref[:, i * DA_QK_DIM:(i + 1) * DA_QK_DIM] * scale).astype(BF16) for i in range(nb)]
    slope = slope_ref[...]

    def alibi(width, base):
        row = lax.broadcasted_iota(jnp.int32, (nb * dec, width), 0)
        col = lax.broadcasted_iota(jnp.int32, (nb * dec, width), 1)
        dist = base + lax.rem(row, dec) - col
        return dist, -slope * dist.astype(F32)

    @pl.when(p == 0)
    def _():
        m_sc[...] = jnp.full(m_sc.shape, NEG_BIG, F32)
        l_sc[...] = jnp.zeros(l_sc.shape, F32)
        a_sc[...] = jnp.zeros(a_sc.shape, F32)
        kb = [[_pad_keys(kn_ref[:, i * DA_QK_DIM:(i + 1) * DA_QK_DIM]).astype(BF16)] for i in range(nb)]
        vb = [[_pad_keys(vn_ref[:, h * DA_V_DIM:(h + 1) * DA_V_DIM]).astype(BF16)]
              for h in range(DA_HEADS)]
        dist, bias = alibi(PAGE_SIZE, 0)
        _attend(qb, kb, vb, bias, dist >= 0, m_sc, l_sc, a_sc)

    kb = [[_page_rows(kc[g], mp * DA_HEADS + h).astype(BF16) for g in range(npg)]
          for h in range(DA_HEADS) for mp in range(2)]
    vb = [[jnp.concatenate([_page_rows(vc[g], h), _page_rows(vc[g], DA_HEADS + h)],
                           axis=1).astype(BF16) for g in range(npg)]
          for h in range(DA_HEADS)]
    _, bias = alibi(npg * PAGE_SIZE, past - p * (npg * PAGE_SIZE))
    _attend(qb, kb, vb, bias, None, m_sc, l_sc, a_sc)

    @pl.when(p == pl.num_programs(1) - 1)
    def _():
        lam = _lambda(lam_ref[...], lam_init)
        for h in range(DA_HEADS):
            r1 = slice((2 * h) * dec, (2 * h + 1) * dec)
            r2 = slice((2 * h + 1) * dec, (2 * h + 2) * dec)
            o = _diff_finish(a_sc[r1, :], l_sc[r1, :], a_sc[r2, :], l_sc[r2, :], lam, g_ref[h], lam_init)
            o_ref[:, h * DA_V_DIM:(h + 1) * DA_V_DIM] = o


def _stored_pages_diff(pool):
    n_phys = pool.shape[0]
    x = pool.reshape(n_phys, PAGE_SIZE, DA_HEADS, 2, LANES)
    return jnp.transpose(x, (0, 1, 3, 2, 4)).reshape(n_phys, PAGE_SIZE * ROWS_PER_KEY, LANES)


def _sample_diff(dq, dk_new, dv_new, pool_k, pool_v, page_table, slopes, lam_vec, g_sub,
                 lam_init, dec, npg=PAGES_PER_STEP):
    n_seq, n_pages = page_table.shape
    npg = math.gcd(npg, n_pages)
    nb = 2 * DA_HEADS
    body = functools.partial(_sdiff_body, dec=dec, past=n_pages * PAGE_SIZE, lam_init=lam_init,
                             npg=npg)
    tok = lambda s, p, pt: (s, 0)
    const2 = lambda s, p, pt: (0, 0)

    def page(g):
        return pl.BlockSpec((1, PAGE_SIZE * ROWS_PER_KEY, LANES),
                            lambda s, p, pt: (pt[s * n_pages + p * npg + g], 0, 0))

    slope_rows = jnp.repeat(slopes, 2 * dec).reshape(nb * dec, 1)
    grid_spec = pltpu.PrefetchScalarGridSpec(
        num_scalar_prefetch=1,
        grid=(n_seq, n_pages // npg),
        in_specs=[
            pl.BlockSpec((nb * dec, 1), const2),
            pl.BlockSpec((4, DA_QK_DIM), const2),
            pl.BlockSpec((dec, DA_WIDTH), tok),
            pl.BlockSpec((dec, DA_WIDTH), tok),
            pl.BlockSpec((dec, DA_WIDTH), tok),
            pl.BlockSpec((DA_HEADS, 1, DA_V_DIM), lambda s, p, pt: (0, 0, 0)),
        ] + [page(g) for g in range(npg)] * 2,
        out_specs=pl.BlockSpec((dec, DA_WIDTH), tok),
        scratch_shapes=[
            pltpu.VMEM((nb * dec, 1), F32),
            pltpu.VMEM((nb * dec, 1), F32),
            pltpu.VMEM((nb * dec, DA_V_DIM), F32),
        ],
    )
    pages_k = _stored_pages_diff(pool_k)
    pages_v = _stored_pages_diff(pool_v)
    return pl.pallas_call(
        body,
        grid_spec=grid_spec,
        out_shape=jax.ShapeDtypeStruct((n_seq * dec, DA_WIDTH), F32),
        compiler_params=_cparams(("parallel", "arbitrary")),
        name="sample_diff",
    )(page_table.reshape(-1), slope_rows, lam_vec, dq, dk_new, dv_new,
      g_sub.reshape(DA_HEADS, 1, DA_V_DIM), *([pages_k] * npg), *([pages_v] * npg))


def _bf16_pieces(x):
    hi = x.astype(BF16).astype(F32)
    r1 = x - hi
    mid = r1.astype(BF16).astype(F32)
    lo = (r1 - mid).astype(BF16).astype(F32)
    return hi, mid, lo


def _sfox_body(pt_ref, q_ref, kn_ref, vn_ref, lfn_ref, *rest, dec, npg):
    kc, vc, lfc = rest[:npg], rest[npg:2 * npg], rest[2 * npg:3 * npg]
    o_ref = rest[3 * npg]
    m_sc, l_sc, a_sc, cq_sc, carry_sc = rest[3 * npg + 1:]
    p = pl.program_id(1)
    scale = FX_DIM ** -0.5
    qb = [(q_ref[:, h * FX_DIM:(h + 1) * FX_DIM] * scale).astype(BF16) for h in range(FX_HEADS)]

    @pl.when(p == 0)
    def _():
        m_sc[...] = jnp.full(m_sc.shape, NEG_BIG, F32)
        l_sc[...] = jnp.zeros(l_sc.shape, F32)
        a_sc[...] = jnp.zeros(a_sc.shape, F32)
        carry_sc[...] = jnp.zeros(carry_sc.shape, F32)
        lfn = lfn_ref[...]
        rows = [lfn[0:1]]
        for t in range(1, dec):
            rows.append(rows[-1] + lfn[t:t + 1])
        r8 = lax.broadcasted_iota(jnp.int32, (dec, FX_HEADS), 0)
        cum = jnp.zeros((dec, FX_HEADS), F32)
        for t in range(dec):
            cum = jnp.where(r8 == t, rows[t], cum)
        cq_sc[...] = cum
        kb = [[_pad_keys(kn_ref[:, h * FX_DIM:(h + 1) * FX_DIM]).astype(BF16)] for h in range(FX_HEADS)]
        vb = [[_pad_keys(vn_ref[:, h * FX_DIM:(h + 1) * FX_DIM]).astype(BF16)] for h in range(FX_HEADS)]
        bias = []
        for h in range(FX_HEADS):
            cqh = cum[:, h:h + 1]
            ck = jnp.concatenate([_col_to_row(cqh), jnp.zeros((1, PAGE_SIZE - dec), F32)], axis=1)
            bias.append(cqh - ck)
        row = lax.broadcasted_iota(jnp.int32, (FX_HEADS * dec, PAGE_SIZE), 0)
        col = lax.broadcasted_iota(jnp.int32, (FX_HEADS * dec, PAGE_SIZE), 1)
        _attend(qb, kb, vb, jnp.concatenate(bias, axis=0), col <= lax.rem(row, dec), m_sc, l_sc, a_sc)

    ki = lax.broadcasted_iota(jnp.int32, (PAGE_SIZE, PAGE_SIZE), 0)
    kj = lax.broadcasted_iota(jnp.int32, (PAGE_SIZE, PAGE_SIZE), 1)
    later = jnp.where(ki > kj, 1.0, 0.0).astype(BF16)
    lf = [lfc[g][0] for g in range(npg)]
    pieces = [_bf16_pieces(x) for x in lf]
    lhs = jnp.concatenate([pc[i] for i in range(3) for pc in pieces], axis=0).astype(BF16)
    prod = jnp.dot(lhs, later, preferred_element_type=F32)
    nh = FX_HEADS
    carry = carry_sc[...]
    after = []
    for g in range(npg):
        inside = (prod[g * nh:(g + 1) * nh] + prod[(npg + g) * nh:(npg + g + 1) * nh]
                  + prod[(2 * npg + g) * nh:(2 * npg + g + 1) * nh])
        after.append(inside + carry)
        carry = carry + inside[:, 0:1] + lf[g][:, 0:1]
    carry_sc[...] = carry
    cq = cq_sc[...]
    bias = jnp.concatenate(
        [cq[:, h:h + 1] + jnp.concatenate([a[h:h + 1, :] for a in after], axis=1)
         for h in range(FX_HEADS)], axis=0)
    kb = [[_page_rows(kc[g], h).astype(BF16) for g in range(npg)] for h in range(FX_HEADS)]
    vb = [[_page_rows(vc[g], h).astype(BF16) for g in range(npg)] for h in range(FX_HEADS)]
    _attend(qb, kb, vb, bias, None, m_sc, l_sc, a_sc)

    @pl.when(p == pl.num_programs(1) - 1)
    def _():
        o = a_sc[...] / l_sc[...]
        for h in range(FX_HEADS):
            o_ref[:, h * FX_DIM:(h + 1) * FX_DIM] = o[h * dec:(h + 1) * dec, :]


def _sample_fox(fq, fk_new, fv_new, lf_new, pool_k, pool_v, pool_lf_t, page_table, dec,
                npg=PAGES_PER_STEP):
    n_seq, n_pages = page_table.shape
    npg = math.gcd(npg, n_pages)
    n_phys = pool_k.shape[0]
    pages_k = pool_k.reshape(n_phys, PAGE_SIZE * ROWS_PER_KEY, LANES)
    pages_v = pool_v.reshape(n_phys, PAGE_SIZE * ROWS_PER_KEY, LANES)
    tok = lambda s, p, pt: (s, 0)

    def page(g, shape):
        return pl.BlockSpec(shape, lambda s, p, pt: (pt[s * n_pages + n_pages - 1 - (p * npg + g)], 0, 0))

    grid_spec = pltpu.PrefetchScalarGridSpec(
        num_scalar_prefetch=1,
        grid=(n_seq, n_pages // npg),
        in_specs=[
            pl.BlockSpec((dec, FX_WIDTH), tok),
            pl.BlockSpec((dec, FX_WIDTH), tok),
            pl.BlockSpec((dec, FX_WIDTH), tok),
            pl.BlockSpec((dec, FX_HEADS), tok),
        ] + [page(g, (1, PAGE_SIZE * ROWS_PER_KEY, LANES)) for g in range(npg)] * 2
          + [page(g, (1, FX_HEADS, PAGE_SIZE)) for g in range(npg)],
        out_specs=pl.BlockSpec((dec, FX_WIDTH), tok),
        scratch_shapes=[
            pltpu.VMEM((FX_HEADS * dec, 1), F32),
            pltpu.VMEM((FX_HEADS * dec, 1), F32),
            pltpu.VMEM((FX_HEADS * dec, FX_DIM), F32),
            pltpu.VMEM((dec, FX_HEADS), F32),
            pltpu.VMEM((FX_HEADS, 1), F32),
        ],
    )
    return pl.pallas_call(
        functools.partial(_sfox_body, dec=dec, npg=npg),
        grid_spec=grid_spec,
        out_shape=jax.ShapeDtypeStruct((n_seq * dec, FX_WIDTH), F32),
        compiler_params=_cparams(("parallel", "arbitrary")),
        name="sample_fox",
    )(page_table.reshape(-1), fq, fk_new, fv_new, lf_new,
      *([pages_k] * npg), *([pages_v] * npg), *([pool_lf_t] * npg))


def _xattn_body(q_ref, k_ref, v_ref, o_ref):
    scale = MEM_DIM ** -0.5
    s = _dot_nt(q_ref[...].astype(BF16), k_ref[0].astype(BF16)) * scale
    s = s - jnp.max(s, axis=-1, keepdims=True)
    e = jnp.exp(s)
    p = e / jnp.sum(e, axis=-1, keepdims=True)
    o_ref[...] = jnp.dot(p.astype(BF16), v_ref[0].astype(BF16), preferred_element_type=F32)


def _cross_attn(q, mem_k, mem_v, tq):
    groups, mem_len, _ = mem_k.shape
    m = q.shape[0]
    nq = m // (groups * tq)
    return pl.pallas_call(
        _xattn_body,
        grid=(groups, nq, MEM_HEADS),
        in_specs=[
            pl.BlockSpec((tq, MEM_DIM), lambda g, i, h: (g * nq + i, h)),
            pl.BlockSpec((1, mem_len, MEM_DIM), lambda g, i, h: (g, 0, h)),
            pl.BlockSpec((1, mem_len, MEM_DIM), lambda g, i, h: (g, 0, h)),
        ],
        out_specs=pl.BlockSpec((tq, MEM_DIM), lambda g, i, h: (g * nq + i, h)),
        out_shape=jax.ShapeDtypeStruct((m, D_MODEL), F32),
        compiler_params=_cparams(("parallel", "parallel", "arbitrary")),
    )(q, mem_k, mem_v)


MEM_ROWS_PER_TOKEN = MEM_HEADS * MEM_DIM // LANES


def _xattn_stored_body(q_ref, k_ref, v_ref, o_ref, *, mem_len):
    scale = MEM_DIM ** -0.5
    groups = MEM_DIM // LANES

    def head_block(ref, h):
        return jnp.concatenate(
            [ref[pl.ds(0, 1), pl.ds(c * MEM_HEADS + h, mem_len, stride=MEM_ROWS_PER_TOKEN), :][0]
             for c in range(groups)], axis=1).astype(BF16)

    for h in range(MEM_HEADS):
        q = (q_ref[:, h * MEM_DIM:(h + 1) * MEM_DIM] * scale).astype(BF16)
        s = _dot_nt(q, head_block(k_ref, h))
        s = s - jnp.max(s, axis=-1, keepdims=True)
        e = jnp.exp(s)
        p = e / jnp.sum(e, axis=-1, keepdims=True)
        o_ref[:, h * MEM_DIM:(h + 1) * MEM_DIM] = jnp.dot(
            p.astype(BF16), head_block(v_ref, h), preferred_element_type=F32)


def _stored_mem(cache):
    n_seq, mem_len = cache.shape[:2]
    x = cache.reshape(n_seq, mem_len, MEM_HEADS, MEM_DIM // LANES, LANES)
    return jnp.transpose(x, (0, 1, 3, 2, 4)).reshape(n_seq, mem_len * MEM_ROWS_PER_TOKEN, LANES)


def _cross_attn_stored(q, cache_k, cache_v, dec):
    n_seq, mem_len = cache_k.shape[:2]
    rows = mem_len * MEM_ROWS_PER_TOKEN
    return pl.pallas_call(
        functools.partial(_xattn_stored_body, mem_len=mem_len),
        grid=(n_seq,),
        in_specs=[
            pl.BlockSpec((dec, D_MODEL), lambda s: (s, 0)),
            pl.BlockSpec((1, rows, LANES), lambda s: (s, 0, 0)),
            pl.BlockSpec((1, rows, LANES), lambda s: (s, 0, 0)),
        ],
        out_specs=pl.BlockSpec((dec, D_MODEL), lambda s: (s, 0)),
        out_shape=jax.ShapeDtypeStruct((n_seq * dec, D_MODEL), F32),
        compiler_params=_cparams(("parallel",)),
        name="sample_cross_attn",
    )(q, _stored_mem(cache_k), _stored_mem(cache_v))


def _cand_layout():
    pos, valid = [], []
    for b in range(16):
        pos.append(b); valid.append(True)
    for a in range(1, 8):
        for b in range(8):
            pos.append(a * 16 + b); valid.append((a + 1) * (b + 1) <= PEER_TOPK)
    for a in range(8, 16):
        pos.append(a * 16); valid.append(True)
    return np.asarray(pos, np.float32)[:, None], np.asarray(valid, np.float32)[:, None]


def _top16(s, iota_f):
    rank = jnp.full(s.shape, float(PEER_TOPK), F32)
    vals = []
    work = s
    for a in range(PEER_TOPK):
        m = jnp.max(work, axis=0, keepdims=True)
        idx = jnp.min(jnp.where(work == m, iota_f, 1e9), axis=0, keepdims=True)
        hit = iota_f == idx
        rank = jnp.where(hit, float(a), rank)
        work = jnp.where(hit, -jnp.inf, work)
        vals.append(m)
    return rank, vals


def _rows_to_array(rows, n):
    tt = rows[0].shape[1]
    r = lax.broadcasted_iota(jnp.int32, (n, tt), 0)
    out = jnp.zeros((n, tt), F32)
    for a in range(n):
        out = jnp.where(r == a, rows[a], out)
    return out


def _psel_body(q_ref, k1_ref, k2_ref, pos_ref, valid_ref, w1_ref, n1_ref, w2_ref, r2_ref):
    tt = q_ref.shape[0]
    iota_f = lax.broadcasted_iota(jnp.int32, (N_KEYS, tt), 0).astype(F32)
    posb = jnp.broadcast_to(pos_ref[...], (pos_ref.shape[0], tt))
    validb = jnp.broadcast_to(valid_ref[...], posb.shape) > 0.5
    i16 = lax.broadcasted_iota(jnp.int32, (PEER_TOPK, tt), 0).astype(F32)
    for h in range(PEER_HEADS):
        lo = h * 2 * PEER_HALF
        qa = q_ref[:, lo:lo + PEER_HALF].astype(BF16)
        qb = q_ref[:, lo + PEER_HALF:lo + 2 * PEER_HALF].astype(BF16)
        s1 = _dot_nt(k1_ref[...], qa)
        s2 = _dot_nt(k2_ref[...], qb)
        rank1, v1 = _top16(s1, iota_f)
        rank2, v2 = _top16(s2, iota_f)
        v1a = _rows_to_array(v1, PEER_TOPK)
        v2a = _rows_to_array(v2, PEER_TOPK)
        blocks = [v1[0] + v2a]
        for a in range(1, 8):
            blocks.append(v1[a] + v2a[0:8])
        blocks.append(v1a[8:16] + v2[0])
        work = jnp.where(validb, jnp.concatenate(blocks, axis=0), -jnp.inf)
        top = v1[0] + v2[0]
        z = jnp.zeros((1, tt), F32)
        cnt = jnp.zeros((PEER_TOPK, tt), F32)
        for _ in range(PEER_TOPK):
            m = jnp.max(work, axis=0, keepdims=True)
            pmin = jnp.min(jnp.where(work == m, posb, 1e9), axis=0, keepdims=True)
            work = jnp.where(posb == pmin, -jnp.inf, work)
            z = z + jnp.exp(m - top)
            cnt = cnt + jnp.where(i16 == jnp.floor(pmin * (1.0 / 16.0)), 1.0, 0.0)
        n1 = jnp.zeros((N_KEYS, tt), F32)
        for a in range(PEER_TOPK):
            n1 = jnp.where(rank1 == float(a), cnt[a:a + 1], n1)
        w1_ref[h] = jnp.where(rank1 < PEER_TOPK, jnp.exp(s1 - v1[0]), 0.0)
        n1_ref[h] = n1
        w2_ref[h] = jnp.where(rank2 < PEER_TOPK, jnp.exp(s2 - v2[0]) / z, 0.0).astype(BF16)
        r2_ref[h] = rank2.astype(BF16)


def _peer_select(q, sub_k1, sub_k2, tt=256):
    m = q.shape[0]
    tt = min(tt, m)
    pos, valid = _cand_layout()
    n_c = pos.shape[0]
    tab = jax.ShapeDtypeStruct((PEER_HEADS, N_KEYS, m), F32)
    tab16 = jax.ShapeDtypeStruct((PEER_HEADS, N_KEYS, m), BF16)
    tab_spec = pl.BlockSpec((PEER_HEADS, N_KEYS, tt), lambda i: (0, 0, i))
    return pl.pallas_call(
        _psel_body,
        grid=(m // tt,),
        in_specs=[
            pl.BlockSpec((tt, q.shape[1]), lambda i: (i, 0)),
            pl.BlockSpec((N_KEYS, PEER_HALF), lambda i: (0, 0)),
            pl.BlockSpec((N_KEYS, PEER_HALF), lambda i: (0, 0)),
            pl.BlockSpec((n_c, 1), lambda i: (0, 0)),
            pl.BlockSpec((n_c, 1), lambda i: (0, 0)),
        ],
        out_specs=[tab_spec] * 4,
        out_shape=[tab, tab, tab16, tab16],
        compiler_params=_cparams(("parallel",)),
        name="peer_select",
    )(q, sub_k1.astype(BF16), sub_k2.astype(BF16), jnp.asarray(pos), jnp.asarray(valid))


def _gelu_tanh(x):
    return 0.5 * x * (1.0 + jnp.tanh(math.sqrt(2.0 / math.pi) * (x + 0.044715 * (x * x * x))))


def _pexp_body(x_ref, xn_ref, gl_ref, u_ref, vt_ref, w1_ref, n1_ref, w2_ref, r2_ref,
               y_ref, act_a, act_b, wt_a, wt_b, acc_sc, *, rows_per_step, tchunk, n_blocks):
    s = pl.program_id(1)
    tt = x_ref.shape[0]

    @pl.when(s == 0)
    def _():
        acc_sc[...] = jnp.zeros(acc_sc.shape, F32)
        for ref in (act_a, act_b, wt_a, wt_b):
            ref[...] = jnp.zeros(ref.shape, ref.dtype)

    gated = jnp.clip(s - 1, 0, n_blocks - 1)

    def stages(act_w, act_r, wt_w, wt_r):
        for rl in range(rows_per_step):
            r = gated * rows_per_step + rl
            for c in range(tt // tchunk):
                cs = slice(c * tchunk, (c + 1) * tchunk)
                gate = jnp.zeros((N_KEYS // 16, 16, tchunk), BF16)
                for h in range(PEER_HEADS):
                    w1 = jnp.broadcast_to(w1_ref[h, pl.ds(r, 1), cs], (16, tchunk)).astype(BF16)
                    n1 = jnp.broadcast_to(n1_ref[h, pl.ds(r, 1), cs], (16, tchunk)).astype(BF16)
                    r2 = r2_ref[h, :, cs].reshape(N_KEYS // 16, 16, tchunk)
                    w2 = w2_ref[h, :, cs].reshape(N_KEYS // 16, 16, tchunk)
                    gate = gate + w1[None] * jnp.where(r2 < n1[None], w2, jnp.zeros_like(w2))
                a = act_r[rl * N_KEYS:(rl + 1) * N_KEYS, cs]
                g32 = gate.reshape(N_KEYS, tchunk).astype(F32)
                wt_w[rl * N_KEYS:(rl + 1) * N_KEYS, cs] = (g32 * _gelu_tanh(a)).astype(BF16)
        acc_sc[...] += jnp.dot(vt_ref[...], wt_r[...], preferred_element_type=F32)
        act_w[...] = _dot_nt(u_ref[...], xn_ref[...])

    @pl.when(s % 2 == 0)
    def _():
        stages(act_a, act_b, wt_b, wt_a)

    @pl.when(s % 2 == 1)
    def _():
        stages(act_b, act_a, wt_a, wt_b)

    @pl.when(s == pl.num_programs(1) - 1)
    def _():
        x3 = x_ref[...] + acc_sc[...].T
        y_ref[...] = _rms(x3, gl_ref[...], NORM_EPS)


PEER_PIPELINE_DEPTH = 3


def _peer_experts(x, xn, g_final, u_bf, vt_bf, tabs, tt=512, rows_per_step=4):
    m, d = x.shape
    tt = min(tt, m)
    n_exp = u_bf.shape[0]
    ne = rows_per_step * N_KEYS
    n_blocks = n_exp // ne
    last = n_blocks - 1
    tab_spec = pl.BlockSpec((PEER_HEADS, N_KEYS, tt), lambda i, s: (0, 0, i))
    body = functools.partial(_pexp_body, rows_per_step=rows_per_step, tchunk=min(256, tt),
                             n_blocks=n_blocks)
    return pl.pallas_call(
        body,
        grid=(m // tt, n_blocks + PEER_PIPELINE_DEPTH - 1),
        in_specs=[
            pl.BlockSpec((tt, d), lambda i, s: (i, 0)),
            pl.BlockSpec((tt, d), lambda i, s: (i, 0)),
            pl.BlockSpec((1, d), lambda i, s: (0, 0)),
            pl.BlockSpec((ne, d), lambda i, s: (jnp.minimum(s, last), 0)),
            pl.BlockSpec((d, ne), lambda i, s: (0, jnp.clip(s - 2, 0, last))),
            tab_spec, tab_spec, tab_spec, tab_spec,
        ],
        out_specs=pl.BlockSpec((tt, d), lambda i, s: (i, 0)),
        out_shape=jax.ShapeDtypeStruct((m, d), F32),
        scratch_shapes=[
            pltpu.VMEM((ne, tt), F32),
            pltpu.VMEM((ne, tt), F32),
            pltpu.VMEM((ne, tt), BF16),
            pltpu.VMEM((ne, tt), BF16),
            pltpu.VMEM((d, tt), F32),
        ],
        compiler_params=_cparams(("parallel", "arbitrary")),
        name="peer_experts",
    )(x, xn, g_final.reshape(1, d), u_bf, vt_bf, *tabs)


def _alibi_slopes(n):
    return jnp.asarray(2.0 ** (-8.0 * np.arange(1, n + 1) / n), dtype=F32)


def _tail(x, od, of, cross_attn, w, u_bf, vt_bf):
    x1, xn1 = _mm([od, of], [w["out_d"], w["out_f"]], res=x, norm_gain=w["norm_mem"])
    q = _mm([xn1], [w["mem_q"]])
    o = cross_attn(q)
    x2, xn2 = _mm([o], [w["mem_o"]], res=x1, norm_gain=w["norm_ffn"])
    pq = _mm([xn2], [w["peer_q"]])
    tabs = _peer_select(pq, w["sub_k1"], w["sub_k2"])
    return _peer_experts(x2, xn2, w["norm_final"], u_bf, vt_bf, tabs)


def kernel(x_prompt, x_sample, mem_prompt, cache_diff_k, cache_diff_v, cache_fox_k, cache_fox_v,
           cache_fox_logf, cache_mem_k, cache_mem_v, page_table, norm_attn, w_in, b_forget,
           lambda_q1, lambda_k1, lambda_q2, lambda_k2, norm_sub, w_out, norm_mem, w_mem_q,
           w_mem_k, w_mem_v, w_mem_o, norm_ffn, w_peer_q, peer_sub_k1, peer_sub_k2, peer_u,
           peer_v, norm_final):
    depth = w_in.shape[0]
    assert depth == 1, "the residual chain below is written for a single layer"
    l = 0
    batch, seq, d = x_prompt.shape
    n_seq, dec, _ = x_sample.shape
    mem_len = mem_prompt.shape[1]
    lam_init = 0.8 - 0.6 * math.exp(-0.3 * l)
    slopes = _alibi_slopes(DA_HEADS)
    lam_vec = jnp.stack([lambda_q1[l], lambda_k1[l], lambda_q2[l], lambda_k2[l]])

    n_main = 6 * 1024
    w_in_bf = w_in[l, :, :n_main].astype(BF16)
    w_fl_bf = jnp.pad(w_in[l, :, n_main:], ((0, 0), (0, LANES - FX_HEADS))).astype(BF16)
    w = dict(
        out_d=w_out[l, :DA_WIDTH].astype(BF16), out_f=w_out[l, DA_WIDTH:].astype(BF16),
        mem_q=w_mem_q[l].astype(BF16), mem_o=w_mem_o[l].astype(BF16),
        peer_q=w_peer_q[l].astype(BF16), norm_mem=norm_mem[l], norm_ffn=norm_ffn[l],
        norm_final=norm_final, sub_k1=peer_sub_k1[l], sub_k2=peer_sub_k2[l])
    u_bf = peer_u[l].astype(BF16)
    vt_bf = peer_v[l].T.astype(BF16)

    xp = x_prompt.reshape(batch * seq, d)
    xs = x_sample.reshape(n_seq * dec, d)

    bias_f = jnp.zeros((LANES,), F32).at[:FX_HEADS].set(b_forget[l])
    (pdq, pdk, pdv, pfq, pfk, pfv), plf = _project(xp, norm_attn[l], w_in_bf, w_fl_bf, bias_f, 6)
    plf = plf[:, :FX_HEADS]
    cum = _cumsum_lanes(jnp.transpose(plf.reshape(batch, seq, FX_HEADS), (0, 2, 1)))
    od = _prompt_diff(pdq, pdk, pdv, slopes, lam_vec, norm_sub[l], batch, seq, lam_init)
    of = _prompt_fox(pfq, pfk, pfv, cum, batch, seq)
    mem2 = mem_prompt.reshape(batch * mem_len, d)
    pmk = _mm([mem2], [w_mem_k[l].astype(BF16)])
    pmv = _mm([mem2], [w_mem_v[l].astype(BF16)])
    prompt_xattn = lambda q: _cross_attn(q, pmk.reshape(batch, mem_len, d),
                                         pmv.reshape(batch, mem_len, d), min(seq, 2048))
    y_prompt = _tail(xp, od, of, prompt_xattn, w, u_bf, vt_bf)

    (sdq, sdk, sdv, sfq, sfk, sfv), slf = _project(xs, norm_attn[l], w_in_bf, w_fl_bf, bias_f, 6)
    slf = slf[:, :FX_HEADS]
    sod = _sample_diff(sdq, sdk, sdv, cache_diff_k[l], cache_diff_v[l], page_table, slopes,
                       lam_vec, norm_sub[l], lam_init, dec)
    pool_lf_t = jnp.transpose(cache_fox_logf[l], (0, 2, 1))
    sof = _sample_fox(sfq, sfk, sfv, slf, cache_fox_k[l], cache_fox_v[l], pool_lf_t,
                      page_table, dec)
    sample_xattn = lambda q: _cross_attn_stored(q, cache_mem_k[l], cache_mem_v[l], dec)
    y_sample = _tail(xs, sod, sof, sample_xattn, w, u_bf, vt_bf)

    return (y_prompt.reshape(batch, seq, d), y_sample.reshape(n_seq, dec, d),
            pdk.reshape(1, batch, seq, DA_HEADS, DA_V_DIM),
            pdv.reshape(1, batch, seq, DA_HEADS, DA_V_DIM),
            pfk.reshape(1, batch, seq, FX_HEADS, FX_DIM),
            pfv.reshape(1, batch, seq, FX_HEADS, FX_DIM),
            plf.reshape(1, batch, seq, FX_HEADS),
            pmk.reshape(1, batch, mem_len, MEM_HEADS, MEM_DIM),
            pmv.reshape(1, batch, mem_len, MEM_HEADS, MEM_DIM),
            sdk.reshape(1, n_seq, dec, DA_HEADS, DA_V_DIM),
            sdv.reshape(1, n_seq, dec, DA_HEADS, DA_V_DIM),
            sfk.reshape(1, n_seq, dec, FX_HEADS, FX_DIM),
            sfv.reshape(1, n_seq, dec, FX_HEADS, FX_DIM),
            slf.reshape(1, n_seq, dec, FX_HEADS))
```
